```python
import jax, jax.numpy as jnp
from jax import lax
import numpy as np

D_MODEL = 1024
BATCH = 8
SEQ = 2048
DEPTH = 2

D_MIX = D_MODEL
HEAD_DIM = 64
RWKV_HEADS = 6
D_RWKV = RWKV_HEADS * HEAD_DIM
DECAY_LORA = 64
AAA_LORA = 64
GATE_LORA = 128
RWKV_LNX_EPS = 64e-5
MOBA_HEADS = 6
D_MOBA = MOBA_HEADS * HEAD_DIM
MOBA_BLOCK = 256
MOBA_TOPK = 3
MOBA_QCHUNK = 32
GMLP_GROUPS = 4
D_GMLP = GMLP_GROUPS * HEAD_DIM
GMLP_CHUNK = 128
D_FF = ((8 * D_MODEL // 3 + 255) // 256) * 256
NORM_EPS = 1e-6

A_COLS = 3 * D_RWKV + DECAY_LORA + AAA_LORA + GATE_LORA
B_COLS = 3 * D_MOBA
C_COLS = 2 * D_GMLP
D_IN = A_COLS + B_COLS + C_COLS

kernel_name = "hymba_rwkv7_moba_gmlp_block"


def rmsnorm(x, g):
    xf = x.astype(jnp.float32)
    y = xf * lax.rsqrt(jnp.mean(xf * xf, axis=-1, keepdims=True) + NORM_EPS)
    return (y * g.astype(jnp.float32)).astype(x.dtype)


def rwkv7_mixer(ya, mu, w0, w2, a0, a2, g2, k_k, k_a, r_k, lnx_g, lnx_b):
    B, S, _ = ya.shape
    H, D = RWKV_HEADS, HEAD_DIM
    prev = jnp.pad(ya, ((0, 0), (1, 0), (0, 0)))[:, :-1]
    ya = ya + mu * (prev - ya)
    r, k, v, wd, ad, gd = jnp.split(
        ya, [D_RWKV, 2 * D_RWKV, 3 * D_RWKV, 3 * D_RWKV + DECAY_LORA,
             3 * D_RWKV + DECAY_LORA + AAA_LORA], axis=-1)
    w = -jax.nn.softplus(-(w0 + jnp.tanh(wd) @ w2).astype(jnp.float32)) - 0.5
    decay = jnp.exp(-jnp.exp(w))
    a = jax.nn.sigmoid((a0 + ad @ a2).astype(jnp.float32))
    g = jax.nn.sigmoid(gd) @ g2
    heads = lambda t: t.astype(jnp.float32).reshape(B, S, H, D)
    kk = heads(k * k_k)
    kk = kk / jnp.maximum(jnp.linalg.norm(kk, axis=-1, keepdims=True), 1e-12)
    k = k * (1.0 + (a - 1.0) * k_a)
    rh, kh, vh, ah, dh = heads(r), heads(k), heads(v), heads(a), heads(decay)

    def step(state, inp):
        r_t, w_t, k_t, v_t, kk_t, a_t = inp
        sa = jnp.einsum('bhij,bhj->bhi', state, -kk_t)
        state = (state * w_t[:, :, None, :]
                 + sa[..., :, None] * (kk_t * a_t)[:, :, None, :]
                 + v_t[..., :, None] * k_t[:, :, None, :])
        return state, jnp.einsum('bhij,bhj->bhi', state, r_t)

    xs = tuple(jnp.moveaxis(t, 1, 0) for t in (rh, dh, kh, vh, kk, ah))
    s0 = jnp.zeros((B, H, D, D), jnp.float32)
    _, y = lax.scan(step, s0, xs)
    y = jnp.moveaxis(y, 0, 1)
    mean = jnp.mean(y, axis=-1, keepdims=True)
    var = jnp.mean(jnp.square(y - mean), axis=-1, keepdims=True)
    y = ((y - mean) * lax.rsqrt(var + RWKV_LNX_EPS)).reshape(B, S, D_RWKV)
    y = y * lnx_g + lnx_b
    bonus = jnp.sum(rh * kh * r_k.astype(jnp.float32), axis=-1, keepdims=True) * vh
    out = (y + bonus.reshape(B, S, D_RWKV)) * g.astype(jnp.float32)
    return out.astype(ya.dtype)


def moba_mixer(q, k, v):
    B, S, _ = q.shape
    H, D = MOBA_HEADS, HEAD_DIM
    q, k, v = (t.reshape(B, S, H, D).transpose(0, 2, 1, 3) for t in (q, k, v))
    nb = -(-S // MOBA_BLOCK)
    pad = nb * MOBA_BLOCK - S
    kb = jnp.pad(k, ((0, 0), (0, 0), (0, pad), (0, 0))).reshape(B, H, nb, MOBA_BLOCK, D)
    vb = jnp.pad(v, ((0, 0), (0, 0), (0, pad), (0, 0))).reshape(B, H, nb, MOBA_BLOCK, D)
    pos = jnp.arange(S)
    qblk = pos // MOBA_BLOCK
    own = jnp.broadcast_to(qblk[None, None, :, None], (B, H, S, 1))
    n_top = min(MOBA_TOPK, nb - 1)
    if n_top > 0:
        kbar = jnp.mean(kb.astype(jnp.float32), axis=3)
        blk_scores = jnp.einsum('bhsd,bhnd->bhsn', q.astype(jnp.float32), kbar)
        past = jnp.arange(nb)[None, :] < qblk[:, None]
        blk_scores = jnp.where(past, blk_scores, -jnp.inf)
        _, top = lax.top_k(blk_scores, n_top)
        sel = jnp.concatenate([top, own], axis=-1)
        sel_valid = jnp.concatenate(
            [top < qblk[None, None, :, None], jnp.ones_like(own, dtype=bool)], axis=-1)
    else:
        sel = own
        sel_valid = jnp.ones_like(own, dtype=bool)
    nsel = sel.shape[-1]
    nq = S // MOBA_QCHUNK
    q_c = jnp.moveaxis(q.reshape(B, H, nq, MOBA_QCHUNK, D), 2, 0)
    sel_c = jnp.moveaxis(sel.reshape(B, H, nq, MOBA_QCHUNK, nsel), 2, 0)
    val_c = jnp.moveaxis(sel_valid.reshape(B, H, nq, MOBA_QCHUNK, nsel), 2, 0)
    pos_c = pos.reshape(nq, MOBA_QCHUNK)
    bi = jnp.arange(B)[:, None, None, None]
    hi = jnp.arange(H)[None, :, None, None]
    scale = HEAD_DIM ** -0.5

    def attend(args):
        qq, ss, vv, pp = args
        kg = kb[bi, hi, ss]
        vg = vb[bi, hi, ss]
        logits = jnp.einsum('bhqd,bhqnkd->bhqnk', qq, kg).astype(jnp.float32) * scale
        kpos = ss[..., None] * MOBA_BLOCK + jnp.arange(MOBA_BLOCK)
        mask = vv[..., None] & (kpos <= pp[None, None, :, None, None])
        logits = jnp.where(mask, logits, -jnp.inf)
        p = jax.nn.softmax(logits.reshape(B, H, MOBA_QCHUNK, nsel * MOBA_BLOCK), axis=-1)
        p = p.reshape(B, H, MOBA_QCHUNK, nsel, MOBA_BLOCK).astype(vg.dtype)
        return jnp.einsum('bhqnk,bhqnkd->bhqd', p, vg)

    o = lax.map(attend, (q_c, sel_c, val_c, pos_c))
    o = jnp.moveaxis(o, 0, 2).reshape(B, H, S, D).transpose(0, 2, 1, 3)
    return o.reshape(B, S, D_MOBA)


def gmlp_mixer(u, v, ln_g, ln_b, w_s, b_s):
    B, S, _ = u.shape
    u = jax.nn.gelu(u)
    v = jax.nn.gelu(v)
    vf = v.astype(jnp.float32)
    mean = jnp.mean(vf, axis=-1, keepdims=True)
    var = jnp.mean(jnp.square(vf - mean), axis=-1, keepdims=True)
    v = (((vf - mean) * lax.rsqrt(var + NORM_EPS)) * ln_g + ln_b).astype(u.dtype)
    nc = S // GMLP_CHUNK
    vc = v.reshape(B, nc, GMLP_CHUNK, GMLP_GROUPS, HEAD_DIM)
    tril = jnp.tril(jnp.ones((GMLP_CHUNK, GMLP_CHUNK), dtype=bool))
    w_causal = jnp.where(tril[None], w_s, 0.0)
    s = jnp.einsum('gts,bcsgd->bctgd', w_causal, vc) + b_s.T[:, :, None]
    return u * s.reshape(B, S, D_GMLP)


def setup_inputs(seed: int = 0) -> dict:
    key = jax.random.key(seed)
    ks = jax.random.split(key, 24)
    n = lambda k, shape, s: jax.random.normal(k, shape, jnp.float32) * s
    L = DEPTH
    return {
        "x": n(ks[0], (BATCH, SEQ, D_MODEL), 1.0),
        "pre_mix_g": 1.0 + n(ks[1], (L, D_MODEL), 0.02),
        "w_in": n(ks[2], (L, D_MODEL, D_IN), D_MODEL ** -0.5),
        "rwkv_mu": jax.random.uniform(ks[3], (L, A_COLS), jnp.float32),
        "rwkv_w0": jax.random.uniform(ks[4], (L, D_RWKV), jnp.float32, -5.0, 1.0),
        "rwkv_w2": n(ks[5], (L, DECAY_LORA, D_RWKV), 0.1),
        "rwkv_a0": n(ks[6], (L, D_RWKV), 0.1),
        "rwkv_a2": n(ks[7], (L, AAA_LORA, D_RWKV), 0.5 * AAA_LORA ** -0.5),
        "rwkv_g2": n(ks[8], (L, GATE_LORA, D_RWKV), GATE_LORA ** -0.5),
        "rwkv_k_k": 0.85 + n(ks[9], (L, D_RWKV), 0.02),
        "rwkv_k_a": 1.0 + n(ks[10], (L, D_RWKV), 0.02),
        "rwkv_r_k": n(ks[11], (L, RWKV_HEADS, HEAD_DIM), 0.1),
        "rwkv_lnx_g": 1.0 + n(ks[12], (L, D_RWKV), 0.02),
        "rwkv_lnx_b": n(ks[13], (L, D_RWKV), 0.02),
        "gmlp_ln_g": 1.0 + n(ks[14], (L, D_GMLP), 0.02),
        "gmlp_ln_b": n(ks[15], (L, D_GMLP), 0.02),
        "gmlp_w_s": n(ks[16], (L, GMLP_GROUPS, GMLP_CHUNK, GMLP_CHUNK), GMLP_CHUNK ** -0.5),
        "gmlp_b_s": 1.0 + n(ks[17], (L, GMLP_GROUPS, GMLP_CHUNK), 0.01),
        "w_out": n(ks[18], (L, D_MIX, D_MODEL), D_MIX ** -0.5),
        "post_mix_g": 1.0 + n(ks[19], (L, D_MODEL), 0.02),
        "pre_ffn_g": 1.0 + n(ks[20], (L, D_MODEL), 0.02),
        "w_ffn_in": n(ks[21], (L, D_MODEL, 2 * D_FF), D_MODEL ** -0.5),
        "w_ffn_out": n(ks[22], (L, D_FF, D_MODEL), D_FF ** -0.5),
        "post_ffn_g": 1.0 + n(ks[23], (L, D_MODEL), 0.02),
    }


def reference(x, pre_mix_g, w_in, rwkv_mu, rwkv_w0, rwkv_w2, rwkv_a0, rwkv_a2, rwkv_g2,
              rwkv_k_k, rwkv_k_a, rwkv_r_k, rwkv_lnx_g, rwkv_lnx_b, gmlp_ln_g, gmlp_ln_b,
              gmlp_w_s, gmlp_b_s, w_out, post_mix_g, pre_ffn_g, w_ffn_in, w_ffn_out,
              post_ffn_g):
    for l in range(DEPTH):
        h = rmsnorm(x, pre_mix_g[l])
        proj = h @ w_in[l]
        ya, yb, yc = jnp.split(proj, [A_COLS, A_COLS + B_COLS], axis=-1)
        a_out = rwkv7_mixer(ya, rwkv_mu[l], rwkv_w0[l], rwkv_w2[l], rwkv_a0[l],
                            rwkv_a2[l], rwkv_g2[l], rwkv_k_k[l], rwkv_k_a[l],
                            rwkv_r_k[l], rwkv_lnx_g[l], rwkv_lnx_b[l])
        qb, kb, vb = jnp.split(yb, 3, axis=-1)
        b_out = moba_mixer(qb, kb, vb)
        u, v = jnp.split(yc, 2, axis=-1)
        c_out = gmlp_mixer(u, v, gmlp_ln_g[l], gmlp_ln_b[l], gmlp_w_s[l], gmlp_b_s[l])
        mix = jnp.concatenate([a_out, b_out, c_out], axis=-1) @ w_out[l]
        x = x + rmsnorm(mix, post_mix_g[l])
        h = rmsnorm(x, pre_ffn_g[l])
        gate, up = jnp.split(h @ w_ffn_in[l], 2, axis=-1)
        f = (jax.nn.silu(gate) * up) @ w_ffn_out[l]
        x = x + rmsnorm(f, post_ffn_g[l])
    return x
```

```python
import functools

import jax
import jax.numpy as jnp
from jax import lax
from jax.experimental import pallas as pl
from jax.experimental.pallas import tpu as pltpu

F32 = jnp.float32
BF16 = jnp.bfloat16

D_MODEL = 1024
HEAD_DIM = 64
RWKV_HEADS = 6
D_RWKV = RWKV_HEADS * HEAD_DIM
DECAY_LORA = 64
AAA_LORA = 64
GATE_LORA = 128
RWKV_LNX_EPS = 64e-5
MOBA_HEADS = 6
D_MOBA = MOBA_HEADS * HEAD_DIM
MOBA_BLOCK = 256
MOBA_TOPK = 3
GMLP_GROUPS = 4
D_GMLP = GMLP_GROUPS * HEAD_DIM
GMLP_CHUNK = 128
D_FF = 2816
NORM_EPS = 1e-6
A_COLS = 3 * D_RWKV + DECAY_LORA + AAA_LORA + GATE_LORA
B_COLS = 3 * D_MOBA
C_COLS = 2 * D_GMLP
D_IN = A_COLS + B_COLS + C_COLS

LANE = 128
RWKV_CHUNK = 64
VMEM_LIMIT = 56 * 1024 * 1024

NN = (((1,), (0,)), ((), ()))
NT = (((1,), (1,)), ((), ()))
TN = (((0,), (0,)), ((), ()))
HI = lax.Precision.HIGHEST


def _dot(a, b, dims=NN, precision=None):
    return lax.dot_general(a, b, dims, precision=precision, preferred_element_type=F32)


def _split3(x):
    hi = x.astype(BF16)
    r1 = x - hi.astype(F32)
    mid = r1.astype(BF16)
    lo = (r1 - mid.astype(F32)).astype(BF16)
    return hi, mid, lo


def _dot_exact_rhs(x, w_bf16):
    hi, mid, lo = _split3(x)
    return _dot(hi, w_bf16) + _dot(mid, w_bf16) + _dot(lo, w_bf16)


def _dot_exact_lhs(w_bf16, x):
    hi, mid, lo = _split3(x)
    return _dot(w_bf16, hi) + _dot(w_bf16, mid) + _dot(w_bf16, lo)


def _sigmoid(x):
    return 1.0 / (1.0 + jnp.exp(-x))


def _rms(x, g):
    return x * lax.rsqrt(jnp.mean(x * x, axis=-1, keepdims=True) + NORM_EPS) * g


def _params(*sem):
    return pltpu.CompilerParams(dimension_semantics=sem, vmem_limit_bytes=VMEM_LIMIT)


def _in_proj_kernel(x_ref, g_ref, w_ref, o_ref):
    h = _rms(x_ref[...], g_ref[...])
    o_ref[...] = _dot(h.astype(BF16), w_ref[...])


def _in_proj(x2, g, w_bf16, tm=256):
    m, d = x2.shape
    n = w_bf16.shape[1]
    return pl.pallas_call(
        _in_proj_kernel,
        grid=(m // tm,),
        in_specs=[
            pl.BlockSpec((tm, d), lambda i: (i, 0)),
            pl.BlockSpec((1, d), lambda i: (0, 0)),
            pl.BlockSpec((d, n), lambda i: (0, 0)),
        ],
        out_specs=pl.BlockSpec((tm, n), lambda i: (i, 0)),
        out_shape=jax.ShapeDtypeStruct((m, n), F32),
        compiler_params=_params("parallel"),
        name="in_proj",
    )(x2, g.reshape(1, d), w_bf16)


def _rwkv_kernel(ya_ref, mu_ref, wlo_ref, w0a0_ref, g2_ref, vecs_ref, tri_ref, hsum_ref, o_ref,
                 state_ref, prev_ref, at_ref, rt_ref, bh_ref, kh_ref, v_ref, cum_ref, y_ref, *, ts):
    nh, hd, c_len = RWKV_HEADS, HEAD_DIM, RWKV_CHUNK

    @pl.when(pl.program_id(1) == 0)
    def _():
        state_ref[...] = jnp.zeros_like(state_ref)
        prev_ref[...] = jnp.zeros_like(prev_ref)

    ya = ya_ref[0]
    row = lax.broadcasted_iota(jnp.int32, ya.shape, 0)
    prev = jnp.where(row == 0, prev_ref[0:1, :], pltpu.roll(ya, 1, axis=0))
    prev_ref[0:1, :] = ya[ts - 1:ts, :]
    xs = ya + mu_ref[...] * (prev - ya)

    r = xs[:, 0:D_RWKV]
    k = xs[:, D_RWKV:2 * D_RWKV]
    v = xs[:, 2 * D_RWKV:3 * D_RWKV]
    wa = xs[:, 3 * D_RWKV:3 * D_RWKV + LANE]
    gd = xs[:, 3 * D_RWKV + LANE:A_COLS]
    lane = lax.broadcasted_iota(jnp.int32, wa.shape, 1)
    wa = jnp.where(lane < DECAY_LORA, jnp.tanh(wa), wa)
    lo = _dot(wa, wlo_ref[...]) + w0a0_ref[...]
    z = -lo[:, 0:D_RWKV]
    softplus = jnp.maximum(z, 0.0) + jnp.log(1.0 + jnp.exp(-jnp.abs(z)))
    logdecay = -jnp.exp(-softplus - 0.5)
    alpha = _sigmoid(lo[:, D_RWKV:2 * D_RWKV])
    gate = _dot(_sigmoid(gd), g2_ref[...])

    k_k = vecs_ref[0:1, :]
    k_a = vecs_ref[1:2, :]
    r_k = vecs_ref[2:3, :]
    lnx_g = vecs_ref[3:4, :]
    lnx_b = vecs_ref[4:5, :]
    hsum = hsum_ref[...]

    kk = k * k_k
    kk = kk / jnp.maximum(jnp.sqrt(_dot_exact_rhs(kk * kk, hsum)), 1e-12)
    k2 = k * (1.0 + (alpha - 1.0) * k_a)
    cum = _dot_exact_lhs(tri_ref[...], logdecay)
    e_neg = jnp.exp(-cum)
    at = -kk * jnp.exp(cum - logdecay)
    rt = r * jnp.exp(cum)
    bh = kk * alpha * e_neg
    kh = k2 * e_neg
    for h in range(nh):
        sl = slice(h * hd, (h + 1) * hd)
        at_ref[h] = at[:, sl]
        rt_ref[h] = rt[:, sl]
        bh_ref[h] = bh[:, sl]
        kh_ref[h] = kh[:, sl]
        v_ref[h] = v[:, sl]
        cum_ref[h] = cum[:, sl]

    ri = lax.broadcasted_iota(jnp.int32, (c_len, c_len), 0)
    ci = lax.broadcasted_iota(jnp.int32, (c_len, c_len), 1)
    strict = ri > ci
    incl = ri >= ci
    eye = ri == ci

    def chunk_body(c, carry):
        r0 = pl.multiple_of(c * c_len, c_len)
        rows = pl.ds(r0, c_len)
        for h in range(nh):
            at_c = at_ref[h, rows, :]
            rt_c = rt_ref[h, rows, :]
            bh_c = bh_ref[h, rows, :]
            kh_c = kh_ref[h, rows, :]
            v_c = v_ref[h, rows, :]
            g_c = jnp.exp(cum_ref[h, pl.ds(r0 + c_len - 1, 1), :])
            ar = jnp.concatenate([at_c, rt_c], axis=0)
            p_b = _dot(ar, bh_c, NT, HI)
            p_k = _dot(ar, kh_c, NT, HI)
            l_ab = jnp.where(strict, p_b[0:c_len], 0.0)
            a_ak = jnp.where(strict, p_k[0:c_len], 0.0)
            a_rb = jnp.where(incl, p_b[c_len:], 0.0)
            a_rk = jnp.where(incl, p_k[c_len:], 0.0)
            x1 = at_c
            x2 = _dot(a_ak, v_c, NN, HI)
            p = l_ab
            n_sq = c_len.bit_length() - 1
            for it in range(n_sq):
                x1 = x1 + _dot(p, x1, NN, HI)
                x2 = x2 + _dot(p, x2, NN, HI)
                if it + 1 < n_sq:
                    p = _dot(p, p, NN, HI)
            r_eff = rt_c + _dot(a_rb, x1, NN, HI)
            y_in = _dot(a_rb, x2, NN, HI) + _dot(a_rk, v_c, NN, HI)
            b_end = bh_c * g_c
            k_end = kh_c * g_c
            g_mat = jnp.where(eye, g_c, 0.0) + _dot(b_end, x1, TN, HI)
            m_in = _dot(b_end, x2, TN, HI) + _dot(k_end, v_c, TN, HI)
            m_old = state_ref[h]
            y_ref[h, rows, :] = _dot(r_eff, m_old, NN, HI) + y_in
            state_ref[h] = _dot(g_mat, m_old, NN, HI) + m_in
        return carry

    lax.fori_loop(0, ts // c_len, chunk_body, 0)

    y = jnp.concatenate([y_ref[h] for h in range(nh)], axis=-1)
    inv_hd = 1.0 / hd
    mean = _dot_exact_rhs(y, hsum) * inv_hd
    yc = y - mean
    var = _dot_exact_rhs(yc * yc, hsum) * inv_hd
    yn = yc * lax.rsqrt(var + RWKV_LNX_EPS) * lnx_g + lnx_b
    bonus = _dot_exact_rhs(r * k2 * r_k, hsum) * v
    o_ref[0] = ((yn + bonus) * gate).astype(o_ref.dtype)


def _rwkv(proj, mu, w0, w2, a0, a2, g2, k_k, k_a, r_k, lnx_g, lnx_b, ts=256):
    b, s, _ = proj.shape
    wlo = jnp.zeros((LANE, 2 * D_RWKV), F32)
    wlo = wlo.at[0:DECAY_LORA, 0:D_RWKV].set(w2).at[DECAY_LORA:LANE, D_RWKV:].set(a2)
    w0a0 = jnp.concatenate([w0, a0]).reshape(1, 2 * D_RWKV)
    vecs = jnp.zeros((8, D_RWKV), F32)
    vecs = vecs.at[0].set(k_k).at[1].set(k_a).at[2].set(r_k.reshape(-1)).at[3].set(lnx_g).at[4].set(lnx_b)
    ti = jnp.arange(ts)
    tri = ((ti[:, None] >= ti[None, :]) & (ti[:, None] // RWKV_CHUNK == ti[None, :] // RWKV_CHUNK)).astype(BF16)
    hi_ = jnp.arange(D_RWKV) // HEAD_DIM
    hsum = (hi_[:, None] == hi_[None, :]).astype(BF16)
    const = lambda shape: pl.BlockSpec(shape, lambda i, j: (0,) * len(shape))
    head_buf = pltpu.VMEM((RWKV_HEADS, ts, HEAD_DIM), F32)
    return pl.pallas_call(
        functools.partial(_rwkv_kernel, ts=ts),
        grid=(b, s // ts),
        in_specs=[
            pl.BlockSpec((1, ts, A_COLS), lambda i, j: (i, j, 0)),
            const((1, A_COLS)),
            const((LANE, 2 * D_RWKV)),
            const((1, 2 * D_RWKV)),
            const((GATE_LORA, D_RWKV)),
            const((8, D_RWKV)),
            const((ts, ts)),
            const((D_RWKV, D_RWKV)),
        ],
        out_specs=pl.BlockSpec((1, ts, D_RWKV), lambda i, j: (i, j, 0)),
        out_shape=jax.ShapeDtypeStruct((b, s, D_RWKV), BF16),
        scratch_shapes=[
            pltpu.VMEM((RWKV_HEADS, HEAD_DIM, HEAD_DIM), F32),
            pltpu.VMEM((8, A_COLS), F32),
            head_buf, head_buf, head_buf, head_buf, head_buf, head_buf, head_buf,
        ],
        compiler_params=_params("parallel", "arbitrary"),
        name="rwkv7",
    )(proj, mu.reshape(1, A_COLS), wlo, w0a0, g2, vecs, tri, hsum)


def _moba_kernel(q_ref, k_ref, v_ref, o_ref, *, s):
    blk = MOBA_BLOCK
    nb = s // blk
    q = q_ref[0]
    k = k_ref[0]
    k16 = k.astype(BF16)
    v16 = v_ref[0].astype(BF16)
    lane = lax.broadcasted_iota(jnp.int32, (1, LANE), 1)
    kbar = jnp.concatenate(
        [jnp.mean(k[n * blk:(n + 1) * blk], axis=0, keepdims=True) for n in range(nb)], axis=0)
    ri = lax.broadcasted_iota(jnp.int32, (blk, blk), 0)
    ci = lax.broadcasted_iota(jnp.int32, (blk, blk), 1)
    causal = ri >= ci
    blk_id = lax.broadcasted_iota(jnp.int32, (blk, nb), 1)
    n_top = min(MOBA_TOPK, nb - 1)
    neg_inf = -jnp.inf

    for qi in range(nb):
        rows = slice(qi * blk, (qi + 1) * blk)
        outs = []
        for hh in range(2):
            head_lanes = (lane >= hh * HEAD_DIM) & (lane < (hh + 1) * HEAD_DIM)
            qh = jnp.where(head_lanes, q[rows], 0.0)
            qh16 = (qh * (HEAD_DIM ** -0.5)).astype(BF16)
            logits = []
            if qi > 0 and n_top > 0:
                sc = _dot(qh, kbar, NT, HI)
                cnt = jnp.zeros((blk, nb), F32)
                for m in range(qi):
                    cm = sc[:, m:m + 1]
                    beats = (cm > sc) | ((cm == sc) & (blk_id > m))
                    cnt = cnt + jnp.where(beats, 1.0, 0.0)
                sel = (cnt < n_top) & (blk_id < qi)
                for n in range(qi):
                    lg = _dot(qh16, k16[n * blk:(n + 1) * blk], NT)
                    logits.append(jnp.where(sel[:, n:n + 1], lg, neg_inf))
            logits.append(jnp.where(causal, _dot(qh16, k16[rows], NT), neg_inf))
            m_row = functools.reduce(jnp.maximum, [jnp.max(lg, axis=-1, keepdims=True) for lg in logits])
            l_row = jnp.zeros((blk, 1), F32)
            acc = jnp.zeros((blk, LANE), F32)
            first = qi + 1 - len(logits)
            for n, lg in enumerate(logits):
                p = jnp.exp(lg - m_row)
                l_row = l_row + jnp.sum(p, axis=-1, keepdims=True)
                kv = first + n
                acc = acc + _dot(p.astype(BF16), v16[kv * blk:(kv + 1) * blk])
            outs.append(acc / l_row)
        o_ref[0, rows, :] = jnp.where(lane < HEAD_DIM, outs[0], outs[1]).astype(o_ref.dtype)


def _moba(proj):
    b, s, _ = proj.shape
    n_pair = D_MOBA // LANE
    base = A_COLS // LANE
    spec = lambda off: pl.BlockSpec((1, s, LANE), lambda i, p: (i, 0, base + off + p))
    return pl.pallas_call(
        functools.partial(_moba_kernel, s=s),
        grid=(b, n_pair),
        in_specs=[spec(0), spec(n_pair), spec(2 * n_pair)],
        out_specs=pl.BlockSpec((1, s, LANE), lambda i, p: (i, 0, p)),
        out_shape=jax.ShapeDtypeStruct((b, s, D_MOBA), BF16),
        compiler_params=_params("parallel", "parallel"),
        name="moba",
    )(proj, proj, proj)


def _gelu(x):
    return 0.5 * x * (1.0 + jnp.tanh(0.7978845608028654 * (x + 0.044715 * (x * x * x))))


def _gmlp_kernel(u_ref, v_ref, lng_ref, lnb_ref, ws_ref, bias_ref, o_ref, *, ts):
    ch = GMLP_CHUNK
    u = _gelu(u_ref[0])
    v = _gelu(v_ref[0])
    mean = jnp.mean(v, axis=-1, keepdims=True)
    vc = v - mean
    var = jnp.mean(vc * vc, axis=-1, keepdims=True)
    vn = ((vc * lax.rsqrt(var + NORM_EPS)) * lng_ref[...] + lnb_ref[...]).astype(BF16)
    ri = lax.broadcasted_iota(jnp.int32, (ch, ch), 0)
    ci = lax.broadcasted_iota(jnp.int32, (ch, ch), 1)
    tril = ri >= ci
    w = [jnp.where(tril, ws_ref[g], 0.0).astype(BF16) for g in range(GMLP_GROUPS)]
    lane = lax.broadcasted_iota(jnp.int32, (1, D_GMLP), 1)
    bias = bias_ref[...]
    for c in range(ts // ch):
        rows = slice(c * ch, (c + 1) * ch)
        vchunk = vn[rows]
        mixed = _dot(w[0], vchunk)
        for g in range(1, GMLP_GROUPS):
            mixed = jnp.where(lane >= g * HEAD_DIM, _dot(w[g], vchunk), mixed)
        o_ref[0, rows, :] = (u[rows] * (mixed + bias)).astype(o_ref.dtype)


def _gmlp(proj, ln_g, ln_b, w_s, b_s, ts=512):
    b, s, _ = proj.shape
    base = (A_COLS + B_COLS) // D_GMLP
    bias = jnp.repeat(b_s.T, HEAD_DIM, axis=1)
    const = lambda shape: pl.BlockSpec(shape, lambda i, j: (0,) * len(shape))
    return pl.pallas_call(
        functools.partial(_gmlp_kernel, ts=ts),
        grid=(b, s // ts),
        in_specs=[
            pl.BlockSpec((1, ts, D_GMLP), lambda i, j: (i, j, base)),
            pl.BlockSpec((1, ts, D_GMLP), lambda i, j: (i, j, base + 1)),
            const((1, D_GMLP)),
            const((1, D_GMLP)),
            const((GMLP_GROUPS, GMLP_CHUNK, GMLP_CHUNK)),
            const((GMLP_CHUNK, D_GMLP)),
        ],
        out_specs=pl.BlockSpec((1, ts, D_GMLP), lambda i, j: (i, j, 0)),
        out_shape=jax.ShapeDtypeStruct((b, s, D_GMLP), BF16),
        compiler_params=_params("parallel", "parallel"),
        name="gmlp",
    )(proj, proj, ln_g.reshape(1, D_GMLP), ln_b.reshape(1, D_GMLP), w_s, bias)


def _out_proj_kernel(a_ref, b_ref, c_ref, wa_ref, wb_ref, wc_ref, x_ref, g_ref, o_ref):
    mix = _dot(a_ref[...], wa_ref[...]) + _dot(b_ref[...], wb_ref[...]) + _dot(c_ref[...], wc_ref[...])
    o_ref[...] = x_ref[...] + _rms(mix, g_ref[...])


def _out_proj(a2, b2, c2, w_bf16, x2, g, tm=512):
    m, d = x2.shape
    wa = w_bf16[0:D_RWKV]
    wb = w_bf16[D_RWKV:D_RWKV + D_MOBA]
    wc = w_bf16[D_RWKV + D_MOBA:]
    row = lambda n: pl.BlockSpec((tm, n), lambda i: (i, 0))
    const = lambda shape: pl.BlockSpec(shape, lambda i: (0,) * len(shape))
    return pl.pallas_call(
        _out_proj_kernel,
        grid=(m // tm,),
        in_specs=[row(D_RWKV), row(D_MOBA), row(D_GMLP), const(wa.shape), const(wb.shape), const(wc.shape),
                  row(d), const((1, d))],
        out_specs=row(d),
        out_shape=jax.ShapeDtypeStruct((m, d), F32),
        compiler_params=_params("parallel"),
        name="out_proj",
    )(a2, b2, c2, wa, wb, wc, x2, g.reshape(1, d))


def _ffn_kernel(x_ref, gpre_ref, wg_ref, wu_ref, wo_ref, gpost_ref, o_ref, h_ref, acc_ref):
    j = pl.program_id(1)

    @pl.when(j == 0)
    def _():
        h_ref[...] = _rms(x_ref[...], gpre_ref[...]).astype(BF16)
        acc_ref[...] = jnp.zeros_like(acc_ref)

    h = h_ref[...]
    gate = _dot(h, wg_ref[...])
    up = _dot(h, wu_ref[...])
    act = (gate * _sigmoid(gate) * up).astype(BF16)
    acc_ref[...] += _dot(act, wo_ref[...])

    @pl.when(j == pl.num_programs(1) - 1)
    def _():
        o_ref[...] = x_ref[...] + _rms(acc_ref[...], gpost_ref[...])


def _ffn(x2, g_pre, w_in_bf16, w_out_bf16, g_post, tm=512, tf=1408):
    m, d = x2.shape
    nj = D_FF // tf
    return pl.pallas_call(
        _ffn_kernel,
        grid=(m // tm, nj),
        in_specs=[
            pl.BlockSpec((tm, d), lambda i, j: (i, 0)),
            pl.BlockSpec((1, d), lambda i, j: (0, 0)),
            pl.BlockSpec((d, tf), lambda i, j: (0, j)),
            pl.BlockSpec((d, tf), lambda i, j: (0, j + nj)),
            pl.BlockSpec((tf, d), lambda i, j: (j, 0)),
            pl.BlockSpec((1, d), lambda i, j: (0, 0)),
        ],
        out_specs=pl.BlockSpec((tm, d), lambda i, j: (i, 0)),
        out_shape=jax.ShapeDtypeStruct((m, d), F32),
        scratch_shapes=[pltpu.VMEM((tm, d), BF16), pltpu.VMEM((tm, d), F32)],
        compiler_params=_params("parallel", "arbitrary"),
        name="ffn",
    )(x2, g_pre.reshape(1, d), w_in_bf16, w_in_bf16, w_out_bf16, g_post.reshape(1, d))


def kernel(x, pre_mix_g, w_in, rwkv_mu, rwkv_w0, rwkv_w2, rwkv_a0, rwkv_a2, rwkv_g2, rwkv_k_k, rwkv_k_a, rwkv_r_k, rwkv_lnx_g, rwkv_lnx_b, gmlp_ln_g, gmlp_ln_b, gmlp_w_s, gmlp_b_s, w_out, post_mix_g, pre_ffn_g, w_ffn_in, w_ffn_out, post_ffn_g):
    b, s, d = x.shape
    m = b * s
    x2 = x.reshape(m, d)
    for l in range(w_in.shape[0]):
        proj = _in_proj(x2, pre_mix_g[l], w_in[l].astype(BF16)).reshape(b, s, D_IN)
        a_out = _rwkv(proj, rwkv_mu[l], rwkv_w0[l], rwkv_w2[l], rwkv_a0[l], rwkv_a2[l], rwkv_g2[l],
                      rwkv_k_k[l], rwkv_k_a[l], rwkv_r_k[l], rwkv_lnx_g[l], rwkv_lnx_b[l])
        b_out = _moba(proj)
        c_out = _gmlp(proj, gmlp_ln_g[l], gmlp_ln_b[l], gmlp_w_s[l], gmlp_b_s[l])
        x2 = _out_proj(a_out.reshape(m, D_RWKV), b_out.reshape(m, D_MOBA), c_out.reshape(m, D_GMLP),
                       w_out[l].astype(BF16), x2, post_mix_g[l])
        x2 = _ffn(x2, pre_ffn_g[l], w_ffn_in[l].astype(BF16), w_ffn_out[l].astype(BF16), post_ffn_g[l])
    return x2.reshape(b, s, d)
```

```python
import functools

import jax
import jax.numpy as jnp
from jax import lax
from jax.experimental import pallas as pl
from jax.experimental.pallas import tpu as pltpu

F32 = jnp.float32
BF16 = jnp.bfloat16

D_MODEL = 1024
HEAD_DIM = 64
RWKV_HEADS = 6
D_RWKV = RWKV_HEADS * HEAD_DIM
DECAY_LORA = 64
AAA_LORA = 64
GATE_LORA = 128
RWKV_LNX_EPS = 64e-5
MOBA_HEADS = 6
D_MOBA = MOBA_HEADS * HEAD_DIM
MOBA_BLOCK = 256
MOBA_TOPK = 3
GMLP_GROUPS = 4
D_GMLP = GMLP_GROUPS * HEAD_DIM
GMLP_CHUNK = 128
D_FF = 2816
NORM_EPS = 1e-6
A_COLS = 3 * D_RWKV + DECAY_LORA + AAA_LORA + GATE_LORA
B_COLS = 3 * D_MOBA
C_COLS = 2 * D_GMLP
D_IN = A_COLS + B_COLS + C_COLS

LANE = 128
RWKV_CHUNK = 64
VMEM_LIMIT = 56 * 1024 * 1024

NN = (((1,), (0,)), ((), ()))
NT = (((1,), (1,)), ((), ()))
TN = (((0,), (0,)), ((), ()))
HI = lax.Precision.HIGHEST


def _dot(a, b, dims=NN, precision=None):
    return lax.dot_general(a, b, dims, precision=precision, preferred_element_type=F32)


def _mm(a, b, dims=NN):
    return _dot(a.astype(BF16), b.astype(BF16), dims)


def _split3(x):
    hi = x.astype(BF16)
    r1 = x - hi.astype(F32)
    mid = r1.astype(BF16)
    lo = (r1 - mid.astype(F32)).astype(BF16)
    return hi, mid, lo


def _dot_exact_rhs(x, w_bf16):
    hi, mid, lo = _split3(x)
    return _dot(hi, w_bf16) + _dot(mid, w_bf16) + _dot(lo, w_bf16)


def _dot_exact_lhs(w_bf16, x):
    hi, mid, lo = _split3(x)
    return _dot(w_bf16, hi) + _dot(w_bf16, mid) + _dot(w_bf16, lo)


def _sigmoid(x):
    return 1.0 / (1.0 + jnp.exp(-x))


def _rms(x, g):
    return x * lax.rsqrt(jnp.mean(x * x, axis=-1, keepdims=True) + NORM_EPS) * g


def _params(*sem):
    return pltpu.CompilerParams(dimension_semantics=sem, vmem_limit_bytes=VMEM_LIMIT)


def _in_proj_kernel(x_ref, g_ref, w_ref, o_ref):
    h = _rms(x_ref[...], g_ref[...])
    o_ref[...] = _dot(h.astype(BF16), w_ref[...])


def _in_proj(x2, g, w_bf16, tm=256):
    m, d = x2.shape
    n = w_bf16.shape[1]
    return pl.pallas_call(
        _in_proj_kernel,
        grid=(m // tm,),
        in_specs=[
            pl.BlockSpec((tm, d), lambda i: (i, 0)),
            pl.BlockSpec((1, d), lambda i: (0, 0)),
            pl.BlockSpec((d, n), lambda i: (0, 0)),
        ],
        out_specs=pl.BlockSpec((tm, n), lambda i: (i, 0)),
        out_shape=jax.ShapeDtypeStruct((m, n), F32),
        compiler_params=_params("parallel"),
        name="in_proj",
    )(x2, g.reshape(1, d), w_bf16)


def _rwkv_kernel(ya_ref, mu_ref, wlo_ref, w0a0_ref, g2_ref, vecs_ref, tri_ref, hsum_ref, o_ref,
                 state_ref, prev_ref, at_ref, rt_ref, bh_ref, kh_ref, v_ref, cum_ref, y_ref, vr_ref, *, ts):
    nh, hd, c_len = RWKV_HEADS, HEAD_DIM, RWKV_CHUNK

    @pl.when(pl.program_id(1) == 0)
    def _():
        state_ref[...] = jnp.zeros_like(state_ref)
        prev_ref[...] = jnp.zeros_like(prev_ref)

    ya = ya_ref[0]
    row = lax.broadcasted_iota(jnp.int32, ya.shape, 0)
    prev = jnp.where(row == 0, prev_ref[0:1, :], pltpu.roll(ya, 1, axis=0))
    prev_ref[0:1, :] = ya[ts - 1:ts, :]
    xs = ya + mu_ref[...] * (prev - ya)

    r = xs[:, 0:D_RWKV]
    k = xs[:, D_RWKV:2 * D_RWKV]
    v = xs[:, 2 * D_RWKV:3 * D_RWKV]
    wa = xs[:, 3 * D_RWKV:3 * D_RWKV + LANE]
    gd = xs[:, 3 * D_RWKV + LANE:A_COLS]
    lane = lax.broadcasted_iota(jnp.int32, wa.shape, 1)
    wa = jnp.where(lane < DECAY_LORA, jnp.tanh(wa), wa)
    lo = _dot(wa, wlo_ref[...]) + w0a0_ref[...]
    z = -lo[:, 0:D_RWKV]
    softplus = jnp.maximum(z, 0.0) + jnp.log(1.0 + jnp.exp(-jnp.abs(z)))
    logdecay = -jnp.exp(-softplus - 0.5)
    alpha = _sigmoid(lo[:, D_RWKV:2 * D_RWKV])
    gate = _dot(_sigmoid(gd), g2_ref[...])

    k_k = vecs_ref[0:1, :]
    k_a = vecs_ref[1:2, :]
    r_k = vecs_ref[2:3, :]
    lnx_g = vecs_ref[3:4, :]
    lnx_b = vecs_ref[4:5, :]
    hsum = hsum_ref[...]

    kk = k * k_k
    kk = kk / jnp.maximum(jnp.sqrt(_dot_exact_rhs(kk * kk, hsum)), 1e-12)
    k2 = k * (1.0 + (alpha - 1.0) * k_a)
    cum = _dot_exact_lhs(tri_ref[...], logdecay)
    e_neg = jnp.exp(-cum)
    at = -kk * jnp.exp(cum - logdecay)
    rt = r * jnp.exp(cum)
    bh = kk * alpha * e_neg
    kh = k2 * e_neg
    zeros_half = jnp.zeros((ts, hd), F32)
    for h in range(nh):
        sl = slice(h * hd, (h + 1) * hd)
        at_ref[h] = at[:, sl]
        rt_ref[h] = rt[:, sl]
        bh_ref[h] = bh[:, sl]
        kh_ref[h] = kh[:, sl]
        v_ref[h] = v[:, sl]
        vr_ref[h] = jnp.concatenate([zeros_half, v[:, sl]], axis=1)
        cum_ref[h] = cum[:, sl]

    ri = lax.broadcasted_iota(jnp.int32, (c_len, 2 * c_len), 0)
    ci = lax.broadcasted_iota(jnp.int32, (c_len, 2 * c_len), 1)
    ci = jnp.where(ci >= c_len, ci - c_len, ci)
    strict = ri > ci
    incl = ri >= ci
    eye = (lax.broadcasted_iota(jnp.int32, (c_len, c_len), 0)
           == lax.broadcasted_iota(jnp.int32, (c_len, c_len), 1))
    zeros_rows = jnp.zeros((c_len, hd), F32)
    n_sq = c_len.bit_length() - 1

    units = [(cc, h) for cc in range(ts // c_len) for h in range(nh)]

    def cut(ref, u):
        cc, h = u
        return ref[h, cc * c_len:(cc + 1) * c_len, :]

    at_c = [cut(at_ref, u) for u in units]
    rt_c = [cut(rt_ref, u) for u in units]
    bh_c = [cut(bh_ref, u) for u in units]
    kh_c = [cut(kh_ref, u) for u in units]
    g_c = [jnp.exp(cum_ref[h, (cc + 1) * c_len - 1:(cc + 1) * c_len, :]) for cc, h in units]
    p_all = [_mm(jnp.concatenate([a, r_], axis=0), jnp.concatenate([b_, k_], axis=0), NT)
             for a, r_, b_, k_ in zip(at_c, rt_c, bh_c, kh_c)]
    top = [jnp.where(strict, pa[0:c_len], 0.0) for pa in p_all]
    bot = [jnp.where(incl, pa[c_len:], 0.0) for pa in p_all]
    x2 = [_mm(t, jnp.concatenate([zeros_rows, cut(v_ref, u)], axis=0)) for t, u in zip(top, units)]
    x = [jnp.concatenate([a, b_], axis=1) for a, b_ in zip(at_c, x2)]
    p = [t[:, 0:c_len] for t in top]
    for it in range(n_sq):
        x = [xi + _mm(pi, xi) for pi, xi in zip(p, x)]
        if it + 1 < n_sq:
            p = [_mm(pi, pi) for pi in p]
    rhs = [jnp.concatenate([xi, cut(vr_ref, u)], axis=0) for xi, u in zip(x, units)]
    res_y = [_mm(b_, r_) for b_, r_ in zip(bot, rhs)]
    res_m = [_mm(jnp.concatenate([b_ * g, k_ * g], axis=0), r_, TN)
             for b_, k_, g, r_ in zip(bh_c, kh_c, g_c, rhs)]
    r_eff = [r_ + ry[:, 0:hd] for r_, ry in zip(rt_c, res_y)]
    g_mat = [jnp.where(eye, g, 0.0) + rm[:, 0:hd] for g, rm in zip(g_c, res_m)]
    state = [state_ref[h] for h in range(nh)]
    for i, (cc, h) in enumerate(units):
        prod = _mm(jnp.concatenate([r_eff[i], g_mat[i]], axis=0), state[h])
        y_ref[h, cc * c_len:(cc + 1) * c_len, :] = prod[0:c_len] + res_y[i][:, hd:]
        state[h] = prod[c_len:] + res_m[i][:, hd:]
    for h in range(nh):
        state_ref[h] = state[h]

    y = jnp.concatenate([y_ref[h] for h in range(nh)], axis=-1)
    inv_hd = 1.0 / hd
    mean = _dot_exact_rhs(y, hsum) * inv_hd
    yc = y - mean
    var = _dot_exact_rhs(yc * yc, hsum) * inv_hd
    yn = yc * lax.rsqrt(var + RWKV_LNX_EPS) * lnx_g + lnx_b
    bonus = _dot_exact_rhs(r * k2 * r_k, hsum) * v
    o_ref[0] = ((yn + bonus) * gate).astype(o_ref.dtype)


def _rwkv(proj, mu, w0, w2, a0, a2, g2, k_k, k_a, r_k, lnx_g, lnx_b, ts=256):
    b, s, _ = proj.shape
    wlo = jnp.zeros((LANE, 2 * D_RWKV), F32)
    wlo = wlo.at[0:DECAY_LORA, 0:D_RWKV].set(w2).at[DECAY_LORA:LANE, D_RWKV:].set(a2)
    w0a0 = jnp.concatenate([w0, a0]).reshape(1, 2 * D_RWKV)
    vecs = jnp.zeros((8, D_RWKV), F32)
    vecs = vecs.at[0].set(k_k).at[1].set(k_a).at[2].set(r_k.reshape(-1)).at[3].set(lnx_g).at[4].set(lnx_b)
    ti = jnp.arange(ts)
    tri = ((ti[:, None] >= ti[None, :]) & (ti[:, None] // RWKV_CHUNK == ti[None, :] // RWKV_CHUNK)).astype(BF16)
    hi_ = jnp.arange(D_RWKV) // HEAD_DIM
    hsum = (hi_[:, None] == hi_[None, :]).astype(BF16)
    const = lambda shape: pl.BlockSpec(shape, lambda i, j: (0,) * len(shape))
    head_buf = pltpu.VMEM((RWKV_HEADS, ts, HEAD_DIM), F32)
    return pl.pallas_call(
        functools.partial(_rwkv_kernel, ts=ts),
        grid=(b, s // ts),
        in_specs=[
            pl.BlockSpec((1, ts, A_COLS), lambda i, j: (i, j, 0)),
            const((1, A_COLS)),
            const((LANE, 2 * D_RWKV)),
            const((1, 2 * D_RWKV)),
            const((GATE_LORA, D_RWKV)),
            const((8, D_RWKV)),
            const((ts, ts)),
            const((D_RWKV, D_RWKV)),
        ],
        out_specs=pl.BlockSpec((1, ts, D_RWKV), lambda i, j: (i, j, 0)),
        out_shape=jax.ShapeDtypeStruct((b, s, D_RWKV), BF16),
        scratch_shapes=[
            pltpu.VMEM((RWKV_HEADS, HEAD_DIM, HEAD_DIM), F32),
            pltpu.VMEM((8, A_COLS), F32),
            head_buf, head_buf, head_buf, head_buf, head_buf, head_buf, head_buf,
            pltpu.VMEM((RWKV_HEADS, ts, 2 * HEAD_DIM), F32),
        ],
        compiler_params=_params("parallel", "arbitrary"),
        name="rwkv7",
    )(proj, mu.reshape(1, A_COLS), wlo, w0a0, g2, vecs, tri, hsum)


def _moba_kernel(q_ref, k_ref, v_ref, o_ref, *, s):
    blk = MOBA_BLOCK
    nb = s // blk
    q = q_ref[0]
    k = k_ref[0]
    k16 = k.astype(BF16)
    v16 = v_ref[0].astype(BF16)
    lane = lax.broadcasted_iota(jnp.int32, (1, LANE), 1)
    kbar = jnp.concatenate(
        [jnp.mean(k[n * blk:(n + 1) * blk], axis=0, keepdims=True) for n in range(nb)], axis=0)
    ri = lax.broadcasted_iota(jnp.int32, (blk, blk), 0)
    ci = lax.broadcasted_iota(jnp.int32, (blk, blk), 1)
    causal = ri >= ci
    blk_id = lax.broadcasted_iota(jnp.int32, (blk, nb), 1)
    n_top = min(MOBA_TOPK, nb - 1)
    neg_inf = -jnp.inf

    for qi in range(nb):
        rows = slice(qi * blk, (qi + 1) * blk)
        outs = []
        for hh in range(2):
            head_lanes = (lane >= hh * HEAD_DIM) & (lane < (hh + 1) * HEAD_DIM)
            qh = jnp.where(head_lanes, q[rows], 0.0)
            qh16 = (qh * (HEAD_DIM ** -0.5)).astype(BF16)
            logits = []
            if qi > 0 and n_top > 0:
                sc = _dot(qh, kbar, NT, HI)
                cnt = jnp.zeros((blk, nb), F32)
                for m in range(qi):
                    cm = sc[:, m:m + 1]
                    beats = (cm > sc) | ((cm == sc) & (blk_id > m))
                    cnt = cnt + jnp.where(beats, 1.0, 0.0)
                sel = (cnt < n_top) & (blk_id < qi)
                for n in range(qi):
                    lg = _dot(qh16, k16[n * blk:(n + 1) * blk], NT)
                    logits.append(jnp.where(sel[:, n:n + 1], lg, neg_inf))
            logits.append(jnp.where(causal, _dot(qh16, k16[rows], NT), neg_inf))
            m_row = functools.reduce(jnp.maximum, [jnp.max(lg, axis=-1, keepdims=True) for lg in logits])
            l_row = jnp.zeros((blk, 1), F32)
            acc = jnp.zeros((blk, LANE), F32)
            first = qi + 1 - len(logits)
            for n, lg in enumerate(logits):
                p = jnp.exp(lg - m_row)
                l_row = l_row + jnp.sum(p, axis=-1, keepdims=True)
                kv = first + n
                acc = acc + _dot(p.astype(BF16), v16[kv * blk:(kv + 1) * blk])
            outs.append(acc / l_row)
        o_ref[0, rows, :] = jnp.where(lane < HEAD_DIM, outs[0], outs[1]).astype(o_ref.dtype)


def _moba(proj):
    b, s, _ = proj.shape
    n_pair = D_MOBA // LANE
    base = A_COLS // LANE
    spec = lambda off: pl.BlockSpec((1, s, LANE), lambda i, p: (i, 0, base + off + p))
    return pl.pallas_call(
        functools.partial(_moba_kernel, s=s),
        grid=(b, n_pair),
        in_specs=[spec(0), spec(n_pair), spec(2 * n_pair)],
        out_specs=pl.BlockSpec((1, s, LANE), lambda i, p: (i, 0, p)),
        out_shape=jax.ShapeDtypeStruct((b, s, D_MOBA), BF16),
        compiler_params=_params("parallel", "parallel"),
        name="moba",
    )(proj, proj, proj)


def _gelu(x):
    return 0.5 * x * (1.0 + jnp.tanh(0.7978845608028654 * (x + 0.044715 * (x * x * x))))


def _gmlp_kernel(u_ref, v_ref, lng_ref, lnb_ref, ws_ref, bias_ref, o_ref, *, ts):
    ch = GMLP_CHUNK
    u = _gelu(u_ref[0])
    v = _gelu(v_ref[0])
    mean = jnp.mean(v, axis=-1, keepdims=True)
    vc = v - mean
    var = jnp.mean(vc * vc, axis=-1, keepdims=True)
    vn = ((vc * lax.rsqrt(var + NORM_EPS)) * lng_ref[...] + lnb_ref[...]).astype(BF16)
    ri = lax.broadcasted_iota(jnp.int32, (ch, ch), 0)
    ci = lax.broadcasted_iota(jnp.int32, (ch, ch), 1)
    tril = ri >= ci
    w = [jnp.where(tril, ws_ref[g], 0.0).astype(BF16) for g in range(GMLP_GROUPS)]
    lane = lax.broadcasted_iota(jnp.int32, (1, D_GMLP), 1)
    bias = bias_ref[...]
    for c in range(ts // ch):
        rows = slice(c * ch, (c + 1) * ch)
        vchunk = vn[rows]
        mixed = _dot(w[0], vchunk)
        for g in range(1, GMLP_GROUPS):
            mixed = jnp.where(lane >= g * HEAD_DIM, _dot(w[g], vchunk), mixed)
        o_ref[0, rows, :] = (u[rows] * (mixed + bias)).astype(o_ref.dtype)


def _gmlp(proj, ln_g, ln_b, w_s, b_s, ts=512):
    b, s, _ = proj.shape
    base = (A_COLS + B_COLS) // D_GMLP
    bias = jnp.repeat(b_s.T, HEAD_DIM, axis=1)
    const = lambda shape: pl.BlockSpec(shape, lambda i, j: (0,) * len(shape))
    return pl.pallas_call(
        functools.partial(_gmlp_kernel, ts=ts),
        grid=(b, s // ts),
        in_specs=[
            pl.BlockSpec((1, ts, D_GMLP), lambda i, j: (i, j, base)),
            pl.BlockSpec((1, ts, D_GMLP), lambda i, j: (i, j, base + 1)),
            const((1, D_GMLP)),
            const((1, D_GMLP)),
            const((GMLP_GROUPS, GMLP_CHUNK, GMLP_CHUNK)),
            const((GMLP_CHUNK, D_GMLP)),
        ],
        out_specs=pl.BlockSpec((1, ts, D_GMLP), lambda i, j: (i, j, 0)),
        out_shape=jax.ShapeDtypeStruct((b, s, D_GMLP), BF16),
        compiler_params=_params("parallel", "parallel"),
        name="gmlp",
    )(proj, proj, ln_g.reshape(1, D_GMLP), ln_b.reshape(1, D_GMLP), w_s, bias)


def _out_proj_kernel(a_ref, b_ref, c_ref, wa_ref, wb_ref, wc_ref, x_ref, g_ref, o_ref):
    mix = _dot(a_ref[...], wa_ref[...]) + _dot(b_ref[...], wb_ref[...]) + _dot(c_ref[...], wc_ref[...])
    o_ref[...] = x_ref[...] + _rms(mix, g_ref[...])


def _out_proj(a2, b2, c2, w_bf16, x2, g, tm=512):
    m, d = x2.shape
    wa = w_bf16[0:D_RWKV]
    wb = w_bf16[D_RWKV:D_RWKV + D_MOBA]
    wc = w_bf16[D_RWKV + D_MOBA:]
    row = lambda n: pl.BlockSpec((tm, n), lambda i: (i, 0))
    const = lambda shape: pl.BlockSpec(shape, lambda i: (0,) * len(shape))
    return pl.pallas_call(
        _out_proj_kernel,
        grid=(m // tm,),
        in_specs=[row(D_RWKV), row(D_MOBA), row(D_GMLP), const(wa.shape), const(wb.shape), const(wc.shape),
                  row(d), const((1, d))],
        out_specs=row(d),
        out_shape=jax.ShapeDtypeStruct((m, d), F32),
        compiler_params=_params("parallel"),
        name="out_proj",
    )(a2, b2, c2, wa, wb, wc, x2, g.reshape(1, d))


def _ffn_kernel(x_ref, gpre_ref, wg_ref, wu_ref, wo_ref, gpost_ref, o_ref, h_ref, acc_ref):
    j = pl.program_id(1)

    @pl.when(j == 0)
    def _():
        h_ref[...] = _rms(x_ref[...], gpre_ref[...]).astype(BF16)
        acc_ref[...] = jnp.zeros_like(acc_ref)

    h = h_ref[...]
    gate = _dot(h, wg_ref[...])
    up = _dot(h, wu_ref[...])
    act = (gate * _sigmoid(gate) * up).astype(BF16)
    acc_ref[...] += _dot(act, wo_ref[...])

    @pl.when(j == pl.num_programs(1) - 1)
    def _():
        o_ref[...] = x_ref[...] + _rms(acc_ref[...], gpost_ref[...])


def _ffn(x2, g_pre, w_in_bf16, w_out_bf16, g_post, tm=512, tf=1408):
    m, d = x2.shape
    nj = D_FF // tf
    return pl.pallas_call(
        _ffn_kernel,
        grid=(m // tm, nj),
        in_specs=[
            pl.BlockSpec((tm, d), lambda i, j: (i, 0)),
            pl.BlockSpec((1, d), lambda i, j: (0, 0)),
            pl.BlockSpec((d, tf), lambda i, j: (0, j)),
            pl.BlockSpec((d, tf), lambda i, j: (0, j + nj)),
            pl.BlockSpec((tf, d), lambda i, j: (j, 0)),
            pl.BlockSpec((1, d), lambda i, j: (0, 0)),
        ],
        out_specs=pl.BlockSpec((tm, d), lambda i, j: (i, 0)),
        out_shape=jax.ShapeDtypeStruct((m, d), F32),
        scratch_shapes=[pltpu.VMEM((tm, d), BF16), pltpu.VMEM((tm, d), F32)],
        compiler_params=_params("parallel", "arbitrary"),
        name="ffn",
    )(x2, g_pre.reshape(1, d), w_in_bf16, w_in_bf16, w_out_bf16, g_post.reshape(1, d))


def kernel(x, pre_mix_g, w_in, rwkv_mu, rwkv_w0, rwkv_w2, rwkv_a0, rwkv_a2, rwkv_g2, rwkv_k_k, rwkv_k_a, rwkv_r_k, rwkv_lnx_g, rwkv_lnx_b, gmlp_ln_g, gmlp_ln_b, gmlp_w_s, gmlp_b_s, w_out, post_mix_g, pre_ffn_g, w_ffn_in, w_ffn_out, post_ffn_g):
    b, s, d = x.shape
    m = b * s
    x2 = x.reshape(m, d)
    for l in range(w_in.shape[0]):
        proj = _in_proj(x2, pre_mix_g[l], w_in[l].astype(BF16)).reshape(b, s, D_IN)
        a_out = _rwkv(proj, rwkv_mu[l], rwkv_w0[l], rwkv_w2[l], rwkv_a0[l], rwkv_a2[l], rwkv_g2[l],
                      rwkv_k_k[l], rwkv_k_a[l], rwkv_r_k[l], rwkv_lnx_g[l], rwkv_lnx_b[l])
        b_out = _moba(proj)
        c_out = _gmlp(proj, gmlp_ln_g[l], gmlp_ln_b[l], gmlp_w_s[l], gmlp_b_s[l])
        x2 = _out_proj(a_out.reshape(m, D_RWKV), b_out.reshape(m, D_MOBA), c_out.reshape(m, D_GMLP),
                       w_out[l].astype(BF16), x2, post_mix_g[l])
        x2 = _ffn(x2, pre_ffn_g[l], w_ffn_in[l].astype(BF16), w_ffn_out[l].astype(BF16), post_ffn_g[l])
    return x2.reshape(b, s, d)
```

```python
import functools

import jax
import jax.numpy as jnp
from jax import lax
from jax.experimental import pallas as pl
from jax.experimental.pallas import tpu as pltpu

F32 = jnp.float32
BF16 = jnp.bfloat16

D_MODEL = 1024
HEAD_DIM = 64
RWKV_HEADS = 6
D_RWKV = RWKV_HEADS * HEAD_DIM
DECAY_LORA = 64
AAA_LORA = 64
GATE_LORA = 128
RWKV_LNX_EPS = 64e-5
MOBA_HEADS = 6
D_MOBA = MOBA_HEADS * HEAD_DIM
MOBA_BLOCK = 256
MOBA_TOPK = 3
GMLP_GROUPS = 4
D_GMLP = GMLP_GROUPS * HEAD_DIM
GMLP_CHUNK = 128
D_FF = 2816
NORM_EPS = 1e-6
A_COLS = 3 * D_RWKV + DECAY_LORA + AAA_LORA + GATE_LORA
B_COLS = 3 * D_MOBA
C_COLS = 2 * D_GMLP
D_IN = A_COLS + B_COLS + C_COLS

LANE = 128
RWKV_CHUNK = 64
VMEM_LIMIT = 56 * 1024 * 1024

NN = (((1,), (0,)), ((), ()))
NT = (((1,), (1,)), ((), ()))
TN = (((0,), (0,)), ((), ()))
HI = lax.Precision.HIGHEST


def _dot(a, b, dims=NN, precision=None):
    return lax.dot_general(a, b, dims, precision=precision, preferred_element_type=F32)


def _mm(a, b, dims=NN):
    return _dot(a.astype(BF16), b.astype(BF16), dims)


def _split2(x):
    hi = x.astype(BF16)
    lo = (x - hi.astype(F32)).astype(BF16)
    return hi, lo


def _dot_exact_rhs(x, w_bf16):
    hi, lo = _split2(x)
    return _dot(hi, w_bf16) + _dot(lo, w_bf16)


def _dot_exact_lhs(w_bf16, x):
    hi, lo = _split2(x)
    return _dot(w_bf16, hi) + _dot(w_bf16, lo)


def _sigmoid(x):
    return 1.0 / (1.0 + jnp.exp(-x))


def _rms(x, g):
    return x * lax.rsqrt(jnp.mean(x * x, axis=-1, keepdims=True) + NORM_EPS) * g


def _params(*sem):
    return pltpu.CompilerParams(dimension_semantics=sem, vmem_limit_bytes=VMEM_LIMIT)


def _in_proj_kernel(x_ref, g_ref, w_ref, o_ref):
    h = _rms(x_ref[...], g_ref[...])
    o_ref[...] = _dot(h.astype(BF16), w_ref[...])


def _in_proj(x2, g, w_bf16, tm=256):
    m, d = x2.shape
    n = w_bf16.shape[1]
    return pl.pallas_call(
        _in_proj_kernel,
        grid=(m // tm,),
        in_specs=[
            pl.BlockSpec((tm, d), lambda i: (i, 0)),
            pl.BlockSpec((1, d), lambda i: (0, 0)),
            pl.BlockSpec((d, n), lambda i: (0, 0)),
        ],
        out_specs=pl.BlockSpec((tm, n), lambda i: (i, 0)),
        out_shape=jax.ShapeDtypeStruct((m, n), F32),
        compiler_params=_params("parallel"),
        name="in_proj",
    )(x2, g.reshape(1, d), w_bf16)


def _rwkv_kernel(ya_ref, mu_ref, wlo_ref, w0a0_ref, g2_ref, vecs_ref, tri_ref, hsum_ref, o_ref,
                 state_ref, prev_ref, at_ref, rt_ref, bh_ref, kh_ref, v_ref, cum_ref, y_ref, vr_ref, *, ts):
    nh, hd, c_len = RWKV_HEADS, HEAD_DIM, RWKV_CHUNK

    @pl.when(pl.program_id(1) == 0)
    def _():
        state_ref[...] = jnp.zeros_like(state_ref)
        prev_ref[...] = jnp.zeros_like(prev_ref)

    ya = ya_ref[0]
    row = lax.broadcasted_iota(jnp.int32, ya.shape, 0)
    prev = jnp.where(row == 0, prev_ref[0:1, :], pltpu.roll(ya, 1, axis=0))
    prev_ref[0:1, :] = ya[ts - 1:ts, :]
    xs = ya + mu_ref[...] * (prev - ya)

    r = xs[:, 0:D_RWKV]
    k = xs[:, D_RWKV:2 * D_RWKV]
    v = xs[:, 2 * D_RWKV:3 * D_RWKV]
    wa = xs[:, 3 * D_RWKV:3 * D_RWKV + LANE]
    gd = xs[:, 3 * D_RWKV + LANE:A_COLS]
    lane = lax.broadcasted_iota(jnp.int32, wa.shape, 1)
    wa = jnp.where(lane < DECAY_LORA, jnp.tanh(wa), wa)
    lo = _dot(wa, wlo_ref[...]) + w0a0_ref[...]
    z = -lo[:, 0:D_RWKV]
    softplus = jnp.maximum(z, 0.0) + jnp.log(1.0 + jnp.exp(-jnp.abs(z)))
    logdecay = -jnp.exp(-softplus - 0.5)
    alpha = _sigmoid(lo[:, D_RWKV:2 * D_RWKV])
    gate = _dot(_sigmoid(gd), g2_ref[...])

    k_k = vecs_ref[0:1, :]
    k_a = vecs_ref[1:2, :]
    r_k = vecs_ref[2:3, :]
    lnx_g = vecs_ref[3:4, :]
    lnx_b = vecs_ref[4:5, :]
    hsum = hsum_ref[...]

    kk = k * k_k
    kk = kk / jnp.maximum(jnp.sqrt(_dot_exact_rhs(kk * kk, hsum)), 1e-12)
    k2 = k * (1.0 + (alpha - 1.0) * k_a)
    cum = _dot_exact_lhs(tri_ref[...], logdecay)
    e_neg = jnp.exp(-cum)
    at = -kk * jnp.exp(cum - logdecay)
    rt = r * jnp.exp(cum)
    bh = kk * alpha * e_neg
    kh = k2 * e_neg
    zeros_half = jnp.zeros((ts, hd), F32)
    for h in range(nh):
        sl = slice(h * hd, (h + 1) * hd)
        at_ref[h] = at[:, sl]
        rt_ref[h] = rt[:, sl]
        bh_ref[h] = bh[:, sl]
        kh_ref[h] = kh[:, sl]
        v_ref[h] = v[:, sl]
        vr_ref[h] = jnp.concatenate([zeros_half, v[:, sl]], axis=1)
        cum_ref[h] = cum[:, sl]

    ri = lax.broadcasted_iota(jnp.int32, (c_len, 2 * c_len), 0)
    ci = lax.broadcasted_iota(jnp.int32, (c_len, 2 * c_len), 1)
    ci = jnp.where(ci >= c_len, ci - c_len, ci)
    strict = ri > ci
    incl = ri >= ci
    eye = (lax.broadcasted_iota(jnp.int32, (c_len, c_len), 0)
           == lax.broadcasted_iota(jnp.int32, (c_len, c_len), 1))
    zeros_rows = jnp.zeros((c_len, hd), F32)
    n_sq = c_len.bit_length() - 1

    units = [(cc, h) for cc in range(ts // c_len) for h in range(nh)]

    def cut(ref, u):
        cc, h = u
        return ref[h, cc * c_len:(cc + 1) * c_len, :]

    at_c = [cut(at_ref, u) for u in units]
    rt_c = [cut(rt_ref, u) for u in units]
    bh_c = [cut(bh_ref, u) for u in units]
    kh_c = [cut(kh_ref, u) for u in units]
    g_c = [jnp.exp(cum_ref[h, (cc + 1) * c_len - 1:(cc + 1) * c_len, :]) for cc, h in units]
    p_all = [_mm(jnp.concatenate([a, r_], axis=0), jnp.concatenate([b_, k_], axis=0), NT)
             for a, r_, b_, k_ in zip(at_c, rt_c, bh_c, kh_c)]
    top = [jnp.where(strict, pa[0:c_len], 0.0) for pa in p_all]
    bot = [jnp.where(incl, pa[c_len:], 0.0) for pa in p_all]
    x2 = [_mm(t, jnp.concatenate([zeros_rows, cut(v_ref, u)], axis=0)) for t, u in zip(top, units)]
    x = [jnp.concatenate([a, b_], axis=1) for a, b_ in zip(at_c, x2)]
    p = [t[:, 0:c_len] for t in top]
    for it in range(n_sq):
        x = [xi + _mm(pi, xi) for pi, xi in zip(p, x)]
        if it + 1 < n_sq:
            p = [_mm(pi, pi) for pi in p]
    rhs = [jnp.concatenate([xi, cut(vr_ref, u)], axis=0) for xi, u in zip(x, units)]
    res_y = [_mm(b_, r_) for b_, r_ in zip(bot, rhs)]
    res_m = [_mm(jnp.concatenate([b_ * g, k_ * g], axis=0), r_, TN)
             for b_, k_, g, r_ in zip(bh_c, kh_c, g_c, rhs)]
    r_eff = [r_ + ry[:, 0:hd] for r_, ry in zip(rt_c, res_y)]
    g_mat = [jnp.where(eye, g, 0.0) + rm[:, 0:hd] for g, rm in zip(g_c, res_m)]
    state = [state_ref[h] for h in range(nh)]
    for i, (cc, h) in enumerate(units):
        prod = _mm(jnp.concatenate([r_eff[i], g_mat[i]], axis=0), state[h])
        y_ref[h, cc * c_len:(cc + 1) * c_len, :] = prod[0:c_len] + res_y[i][:, hd:]
        state[h] = prod[c_len:] + res_m[i][:, hd:]
    for h in range(nh):
        state_ref[h] = state[h]

    y = jnp.concatenate([y_ref[h] for h in range(nh)], axis=-1)
    inv_hd = 1.0 / hd
    mean = _dot_exact_rhs(y, hsum) * inv_hd
    yc = y - mean
    var = _dot_exact_rhs(yc * yc, hsum) * inv_hd
    yn = yc * lax.rsqrt(var + RWKV_LNX_EPS) * lnx_g + lnx_b
    bonus = _dot_exact_rhs(r * k2 * r_k, hsum) * v
    o_ref[0] = ((yn + bonus) * gate).astype(o_ref.dtype)


def _rwkv(proj, mu, w0, w2, a0, a2, g2, k_k, k_a, r_k, lnx_g, lnx_b, ts=256):
    b, s, _ = proj.shape
    wlo = jnp.zeros((LANE, 2 * D_RWKV), F32)
    wlo = wlo.at[0:DECAY_LORA, 0:D_RWKV].set(w2).at[DECAY_LORA:LANE, D_RWKV:].set(a2)
    w0a0 = jnp.concatenate([w0, a0]).reshape(1, 2 * D_RWKV)
    vecs = jnp.zeros((8, D_RWKV), F32)
    vecs = vecs.at[0].set(k_k).at[1].set(k_a).at[2].set(r_k.reshape(-1)).at[3].set(lnx_g).at[4].set(lnx_b)
    ti = jnp.arange(ts)
    tri = ((ti[:, None] >= ti[None, :]) & (ti[:, None] // RWKV_CHUNK == ti[None, :] // RWKV_CHUNK)).astype(BF16)
    hi_ = jnp.arange(D_RWKV) // HEAD_DIM
    hsum = (hi_[:, None] == hi_[None, :]).astype(BF16)
    const = lambda shape: pl.BlockSpec(shape, lambda i, j: (0,) * len(shape))
    head_buf = pltpu.VMEM((RWKV_HEADS, ts, HEAD_DIM), F32)
    return pl.pallas_call(
        functools.partial(_rwkv_kernel, ts=ts),
        grid=(b, s // ts),
        in_specs=[
            pl.BlockSpec((1, ts, A_COLS), lambda i, j: (i, j, 0)),
            const((1, A_COLS)),
            const((LANE, 2 * D_RWKV)),
            const((1, 2 * D_RWKV)),
            const((GATE_LORA, D_RWKV)),
            const((8, D_RWKV)),
            const((ts, ts)),
            const((D_RWKV, D_RWKV)),
        ],
        out_specs=pl.BlockSpec((1, ts, D_RWKV), lambda i, j: (i, j, 0)),
        out_shape=jax.ShapeDtypeStruct((b, s, D_RWKV), BF16),
        scratch_shapes=[
            pltpu.VMEM((RWKV_HEADS, HEAD_DIM, HEAD_DIM), F32),
            pltpu.VMEM((8, A_COLS), F32),
            head_buf, head_buf, head_buf, head_buf, head_buf, head_buf, head_buf,
            pltpu.VMEM((RWKV_HEADS, ts, 2 * HEAD_DIM), F32),
        ],
        compiler_params=_params("parallel", "arbitrary"),
        name="rwkv7",
    )(proj, mu.reshape(1, A_COLS), wlo, w0a0, g2, vecs, tri, hsum)


def _sublane_allreduce(x, op):
    for shift in (4, 2, 1):
        x = op(x, pltpu.roll(x, shift, axis=0))
    return x


def _moba_kernel(q_ref, k_ref, v_ref, o_ref, *, s):
    blk = MOBA_BLOCK
    nb = s // blk
    sub = 8
    q = q_ref[0]
    k = k_ref[0]
    k16 = k.astype(BF16)
    vt16 = v_ref[0].T.astype(BF16)
    lane = lax.broadcasted_iota(jnp.int32, (1, LANE), 1)
    kbar = jnp.concatenate(
        [jnp.mean(k[n * blk:(n + 1) * blk], axis=0, keepdims=True) for n in range(nb)], axis=0)
    key_i = lax.broadcasted_iota(jnp.int32, (blk, blk), 0)
    qry_i = lax.broadcasted_iota(jnp.int32, (blk, blk), 1)
    neg_inf = -jnp.inf
    causal_pen = jnp.where(key_i <= qry_i, 0.0, neg_inf).reshape(blk // sub, sub, blk)
    blk_id = lax.broadcasted_iota(jnp.int32, (nb, blk), 0)
    out_row = lax.broadcasted_iota(jnp.int32, (LANE, blk), 0)
    n_top = min(MOBA_TOPK, nb - 1)

    def logits3(n, qh16):
        lg = _dot(k16[n * blk:(n + 1) * blk], qh16, NT)
        return lg.reshape(blk // sub, sub, blk)

    for qi in range(nb):
        rows = slice(qi * blk, (qi + 1) * blk)
        outs = []
        for hh in range(2):
            head_lanes = (lane >= hh * HEAD_DIM) & (lane < (hh + 1) * HEAD_DIM)
            qh = jnp.where(head_lanes, q[rows], 0.0)
            qh16 = (qh * (HEAD_DIM ** -0.5)).astype(BF16)
            logits = []
            if qi > 0 and n_top > 0:
                sc = _dot(kbar, qh, NT, HI)
                cnt = jnp.zeros((nb, blk), F32)
                for m in range(qi):
                    cm = sc[m:m + 1, :]
                    beats = (cm > sc) | ((cm == sc) & (blk_id > m))
                    cnt = cnt + jnp.where(beats, 1.0, 0.0)
                pen = jnp.where((cnt < n_top) & (blk_id < qi), 0.0, neg_inf)
                for n in range(qi):
                    pen_n = jnp.broadcast_to(pen[n:n + 1, :], (sub, blk))
                    logits.append(logits3(n, qh16) + pen_n[None])
            logits.append(logits3(qi, qh16) + causal_pen)
            m_rep = functools.reduce(jnp.maximum, [jnp.max(lg, axis=0) for lg in logits])
            m_rep = _sublane_allreduce(m_rep, jnp.maximum)
            l_rep = jnp.zeros((sub, blk), F32)
            acc = jnp.zeros((LANE, blk), F32)
            first = qi + 1 - len(logits)
            for n, lg in enumerate(logits):
                p = jnp.exp(lg - m_rep[None])
                l_rep = l_rep + jnp.sum(p, axis=0)
                kv = first + n
                acc = acc + _dot(vt16[:, kv * blk:(kv + 1) * blk], p.reshape(blk, blk).astype(BF16))
            l_rep = _sublane_allreduce(l_rep, jnp.add)
            outs.append((acc.reshape(LANE // sub, sub, blk) / l_rep[None]).reshape(LANE, blk))
        out_t = jnp.where(out_row < HEAD_DIM, outs[0], outs[1])
        o_ref[0, rows, :] = out_t.T.astype(o_ref.dtype)


def _moba(proj):
    b, s, _ = proj.shape
    n_pair = D_MOBA // LANE
    base = A_COLS // LANE
    spec = lambda off: pl.BlockSpec((1, s, LANE), lambda i, p: (i, 0, base + off + p))
    return pl.pallas_call(
        functools.partial(_moba_kernel, s=s),
        grid=(b, n_pair),
        in_specs=[spec(0), spec(n_pair), spec(2 * n_pair)],
        out_specs=pl.BlockSpec((1, s, LANE), lambda i, p: (i, 0, p)),
        out_shape=jax.ShapeDtypeStruct((b, s, D_MOBA), BF16),
        compiler_params=_params("parallel", "parallel"),
        name="moba",
    )(proj, proj, proj)


def _gelu(x):
    return 0.5 * x * (1.0 + jnp.tanh(0.7978845608028654 * (x + 0.044715 * (x * x * x))))


def _gmlp_kernel(u_ref, v_ref, lng_ref, lnb_ref, ws_ref, bias_ref, o_ref, *, ts):
    ch = GMLP_CHUNK
    u = _gelu(u_ref[0])
    v = _gelu(v_ref[0])
    mean = jnp.mean(v, axis=-1, keepdims=True)
    vc = v - mean
    var = jnp.mean(vc * vc, axis=-1, keepdims=True)
    vn = ((vc * lax.rsqrt(var + NORM_EPS)) * lng_ref[...] + lnb_ref[...]).astype(BF16)
    ri = lax.broadcasted_iota(jnp.int32, (ch, ch), 0)
    ci = lax.broadcasted_iota(jnp.int32, (ch, ch), 1)
    tril = ri >= ci
    w = [jnp.where(tril, ws_ref[g], 0.0).astype(BF16) for g in range(GMLP_GROUPS)]
    lane = lax.broadcasted_iota(jnp.int32, (1, D_GMLP), 1)
    bias = bias_ref[...]
    for c in range(ts // ch):
        rows = slice(c * ch, (c + 1) * ch)
        vchunk = vn[rows]
        mixed = _dot(w[0], vchunk)
        for g in range(1, GMLP_GROUPS):
            mixed = jnp.where(lane >= g * HEAD_DIM, _dot(w[g], vchunk), mixed)
        o_ref[0, rows, :] = (u[rows] * (mixed + bias)).astype(o_ref.dtype)


def _gmlp(proj, ln_g, ln_b, w_s, b_s, ts=512):
    b, s, _ = proj.shape
    base = (A_COLS + B_COLS) // D_GMLP
    bias = jnp.repeat(b_s.T, HEAD_DIM, axis=1)
    const = lambda shape: pl.BlockSpec(shape, lambda i, j: (0,) * len(shape))
    return pl.pallas_call(
        functools.partial(_gmlp_kernel, ts=ts),
        grid=(b, s // ts),
        in_specs=[
            pl.BlockSpec((1, ts, D_GMLP), lambda i, j: (i, j, base)),
            pl.BlockSpec((1, ts, D_GMLP), lambda i, j: (i, j, base + 1)),
            const((1, D_GMLP)),
            const((1, D_GMLP)),
            const((GMLP_GROUPS, GMLP_CHUNK, GMLP_CHUNK)),
            const((GMLP_CHUNK, D_GMLP)),
        ],
        out_specs=pl.BlockSpec((1, ts, D_GMLP), lambda i, j: (i, j, 0)),
        out_shape=jax.ShapeDtypeStruct((b, s, D_GMLP), BF16),
        compiler_params=_params("parallel", "parallel"),
        name="gmlp",
    )(proj, proj, ln_g.reshape(1, D_GMLP), ln_b.reshape(1, D_GMLP), w_s, bias)


def _out_proj_kernel(a_ref, b_ref, c_ref, wa_ref, wb_ref, wc_ref, x_ref, g_ref, o_ref):
    mix = _dot(a_ref[...], wa_ref[...]) + _dot(b_ref[...], wb_ref[...]) + _dot(c_ref[...], wc_ref[...])
    o_ref[...] = x_ref[...] + _rms(mix, g_ref[...])


def _out_proj(a2, b2, c2, w_bf16, x2, g, tm=512):
    m, d = x2.shape
    wa = w_bf16[0:D_RWKV]
    wb = w_bf16[D_RWKV:D_RWKV + D_MOBA]
    wc = w_bf16[D_RWKV + D_MOBA:]
    row = lambda n: pl.BlockSpec((tm, n), lambda i: (i, 0))
    const = lambda shape: pl.BlockSpec(shape, lambda i: (0,) * len(shape))
    return pl.pallas_call(
        _out_proj_kernel,
        grid=(m // tm,),
        in_specs=[row(D_RWKV), row(D_MOBA), row(D_GMLP), const(wa.shape), const(wb.shape), const(wc.shape),
                  row(d), const((1, d))],
        out_specs=row(d),
        out_shape=jax.ShapeDtypeStruct((m, d), F32),
        compiler_params=_params("parallel"),
        name="out_proj",
    )(a2, b2, c2, wa, wb, wc, x2, g.reshape(1, d))


def _ffn_kernel(x_ref, gpre_ref, wg_ref, wu_ref, wo_ref, gpost_ref, o_ref, h_ref, acc_ref):
    j = pl.program_id(1)

    @pl.when(j == 0)
    def _():
        h_ref[...] = _rms(x_ref[...], gpre_ref[...]).astype(BF16)
        acc_ref[...] = jnp.zeros_like(acc_ref)

    h = h_ref[...]
    gate = _dot(h, wg_ref[...])
    up = _dot(h, wu_ref[...])
    act = (gate * _sigmoid(gate) * up).astype(BF16)
    acc_ref[...] += _dot(act, wo_ref[...])

    @pl.when(j == pl.num_programs(1) - 1)
    def _():
        o_ref[...] = x_ref[...] + _rms(acc_ref[...], gpost_ref[...])


def _ffn(x2, g_pre, w_in_bf16, w_out_bf16, g_post, tm=512, tf=1408):
    m, d = x2.shape
    nj = D_FF // tf
    return pl.pallas_call(
        _ffn_kernel,
        grid=(m // tm, nj),
        in_specs=[
            pl.BlockSpec((tm, d), lambda i, j: (i, 0)),
            pl.BlockSpec((1, d), lambda i, j: (0, 0)),
            pl.BlockSpec((d, tf), lambda i, j: (0, j)),
            pl.BlockSpec((d, tf), lambda i, j: (0, j + nj)),
            pl.BlockSpec((tf, d), lambda i, j: (j, 0)),
            pl.BlockSpec((1, d), lambda i, j: (0, 0)),
        ],
        out_specs=pl.BlockSpec((tm, d), lambda i, j: (i, 0)),
        out_shape=jax.ShapeDtypeStruct((m, d), F32),
        scratch_shapes=[pltpu.VMEM((tm, d), BF16), pltpu.VMEM((tm, d), F32)],
        compiler_params=_params("parallel", "arbitrary"),
        name="ffn",
    )(x2, g_pre.reshape(1, d), w_in_bf16, w_in_bf16, w_out_bf16, g_post.reshape(1, d))


def kernel(x, pre_mix_g, w_in, rwkv_mu, rwkv_w0, rwkv_w2, rwkv_a0, rwkv_a2, rwkv_g2, rwkv_k_k, rwkv_k_a, rwkv_r_k, rwkv_lnx_g, rwkv_lnx_b, gmlp_ln_g, gmlp_ln_b, gmlp_w_s, gmlp_b_s, w_out, post_mix_g, pre_ffn_g, w_ffn_in, w_ffn_out, post_ffn_g):
    b, s, d = x.shape
    m = b * s
    x2 = x.reshape(m, d)
    for l in range(w_in.shape[0]):
        proj = _in_proj(x2, pre_mix_g[l], w_in[l].astype(BF16)).reshape(b, s, D_IN)
        a_out = _rwkv(proj, rwkv_mu[l], rwkv_w0[l], rwkv_w2[l], rwkv_a0[l], rwkv_a2[l], rwkv_g2[l],
                      rwkv_k_k[l], rwkv_k_a[l], rwkv_r_k[l], rwkv_lnx_g[l], rwkv_lnx_b[l])
        b_out = _moba(proj)
        c_out = _gmlp(proj, gmlp_ln_g[l], gmlp_ln_b[l], gmlp_w_s[l], gmlp_b_s[l])
        x2 = _out_proj(a_out.reshape(m, D_RWKV), b_out.reshape(m, D_MOBA), c_out.reshape(m, D_GMLP),
                       w_out[l].astype(BF16), x2, post_mix_g[l])
        x2 = _ffn(x2, pre_ffn_g[l], w_ffn_in[l].astype(BF16), w_ffn_out[l].astype(BF16), post_ffn_g[l])
    return x2.reshape(b, s, d)
```

```python
import functools

import jax
import jax.numpy as jnp
from jax import lax
from jax.experimental import pallas as pl
from jax.experimental.pallas import tpu as pltpu

F32 = jnp.float32
BF16 = jnp.bfloat16

D_MODEL = 1024
HEAD_DIM = 64
RWKV_HEADS = 6
D_RWKV = RWKV_HEADS * HEAD_DIM
DECAY_LORA = 64
AAA_LORA = 64
GATE_LORA = 128
RWKV_LNX_EPS = 64e-5
MOBA_HEADS = 6
D_MOBA = MOBA_HEADS * HEAD_DIM
MOBA_BLOCK = 256
MOBA_TOPK = 3
GMLP_GROUPS = 4
D_GMLP = GMLP_GROUPS * HEAD_DIM
GMLP_CHUNK = 128
D_FF = 2816
NORM_EPS = 1e-6
A_COLS = 3 * D_RWKV + DECAY_LORA + AAA_LORA + GATE_LORA
B_COLS = 3 * D_MOBA
C_COLS = 2 * D_GMLP
D_IN = A_COLS + B_COLS + C_COLS

LANE = 128
RWKV_CHUNK = 64
VMEM_LIMIT = 56 * 1024 * 1024

NN = (((1,), (0,)), ((), ()))
NT = (((1,), (1,)), ((), ()))
TN = (((0,), (0,)), ((), ()))
HI = lax.Precision.HIGHEST


def _dot(a, b, dims=NN, precision=None):
    return lax.dot_general(a, b, dims, precision=precision, preferred_element_type=F32)


def _mm(a, b, dims=NN):
    return _dot(a.astype(BF16), b.astype(BF16), dims)


def _split2(x):
    hi = x.astype(BF16)
    lo = (x - hi.astype(F32)).astype(BF16)
    return hi, lo


def _dot_exact_lhs(w_bf16, x):
    hi, lo = _split2(x)
    return _dot(w_bf16, hi) + _dot(w_bf16, lo)


def _sigmoid(x):
    return 1.0 / (1.0 + jnp.exp(-x))


def _rms(x, g):
    return x * lax.rsqrt(jnp.mean(x * x, axis=-1, keepdims=True) + NORM_EPS) * g


def _params(*sem):
    return pltpu.CompilerParams(dimension_semantics=sem, vmem_limit_bytes=VMEM_LIMIT)


def _in_proj_kernel(x_ref, g_ref, w_ref, o_ref, *, tn):
    h = _rms(x_ref[...], g_ref[...]).astype(BF16)
    for c in range(w_ref.shape[1] // tn):
        o_ref[:, c * tn:(c + 1) * tn] = _dot(h, w_ref[:, c * tn:(c + 1) * tn])


def _in_proj(x2, g, w_bf16, tm=512, tn=512):
    m, d = x2.shape
    n = w_bf16.shape[1]
    return pl.pallas_call(
        functools.partial(_in_proj_kernel, tn=tn),
        grid=(m // tm,),
        in_specs=[
            pl.BlockSpec((tm, d), lambda i: (i, 0)),
            pl.BlockSpec((1, d), lambda i: (0, 0), pipeline_mode=pl.Buffered(1)),
            pl.BlockSpec((d, n), lambda i: (0, 0), pipeline_mode=pl.Buffered(1)),
        ],
        out_specs=pl.BlockSpec((tm, n), lambda i: (i, 0)),
        out_shape=jax.ShapeDtypeStruct((m, n), F32),
        compiler_params=_params("parallel"),
        name="in_proj",
    )(x2, g.reshape(1, d), w_bf16)


def _rwkv_kernel(ya_ref, mu_ref, wlo_ref, w0a0_ref, g2_ref, vecs_ref, tri_ref, hsum_ref, o_ref,
                 state_ref, prev_ref, at_ref, rt_ref, bh_ref, kh_ref, v_ref, cum_ref, y_ref, vr_ref, *, ts):
    nh, hd, c_len = RWKV_HEADS, HEAD_DIM, RWKV_CHUNK

    @pl.when(pl.program_id(1) == 0)
    def _():
        state_ref[...] = jnp.zeros_like(state_ref)
        prev_ref[...] = jnp.zeros_like(prev_ref)

    ya = ya_ref[0]
    row = lax.broadcasted_iota(jnp.int32, ya.shape, 0)
    prev = jnp.where(row == 0, prev_ref[0:1, :], pltpu.roll(ya, 1, axis=0))
    prev_ref[0:1, :] = ya[ts - 1:ts, :]
    xs = ya + mu_ref[...] * (prev - ya)

    r = xs[:, 0:D_RWKV]
    k = xs[:, D_RWKV:2 * D_RWKV]
    v = xs[:, 2 * D_RWKV:3 * D_RWKV]
    wa = xs[:, 3 * D_RWKV:3 * D_RWKV + LANE]
    gd = xs[:, 3 * D_RWKV + LANE:A_COLS]
    lane = lax.broadcasted_iota(jnp.int32, wa.shape, 1)
    wa = jnp.where(lane < DECAY_LORA, jnp.tanh(wa), wa)
    lo = _mm(wa, wlo_ref[...]) + w0a0_ref[...]
    z = -lo[:, 0:D_RWKV]
    softplus = jnp.maximum(z, 0.0) + jnp.log(1.0 + jnp.exp(-jnp.abs(z)))
    logdecay = -jnp.exp(-softplus - 0.5)
    alpha = _sigmoid(lo[:, D_RWKV:2 * D_RWKV])
    gate = _mm(_sigmoid(gd), g2_ref[...])

    k_k = vecs_ref[0:1, :]
    k_a = vecs_ref[1:2, :]
    r_k = vecs_ref[2:3, :]
    lnx_g = vecs_ref[3:4, :]
    lnx_b = vecs_ref[4:5, :]
    hsum = hsum_ref[...]

    kk = k * k_k
    kk = kk / jnp.maximum(jnp.sqrt(_mm(kk * kk, hsum)), 1e-12)
    k2 = k * (1.0 + (alpha - 1.0) * k_a)
    tri = tri_ref[...]
    tr = tri.shape[0]
    cum = jnp.concatenate([_dot_exact_lhs(tri, logdecay[i * tr:(i + 1) * tr]) for i in range(ts // tr)],
                          axis=0)
    e_neg = jnp.exp(-cum)
    at = -kk * jnp.exp(cum - logdecay)
    rt = r * jnp.exp(cum)
    bh = kk * alpha * e_neg
    kh = k2 * e_neg
    zeros_half = jnp.zeros((ts, hd), F32)
    for h in range(nh):
        sl = slice(h * hd, (h + 1) * hd)
        at_ref[h] = at[:, sl]
        rt_ref[h] = rt[:, sl]
        bh_ref[h] = bh[:, sl]
        kh_ref[h] = kh[:, sl]
        v_ref[h] = v[:, sl]
        vr_ref[h] = jnp.concatenate([zeros_half, v[:, sl]], axis=1)
        cum_ref[h] = cum[:, sl]

    ri = lax.broadcasted_iota(jnp.int32, (c_len, 2 * c_len), 0)
    ci = lax.broadcasted_iota(jnp.int32, (c_len, 2 * c_len), 1)
    ci = jnp.where(ci >= c_len, ci - c_len, ci)
    strict = ri > ci
    incl = ri >= ci
    eye = (lax.broadcasted_iota(jnp.int32, (c_len, c_len), 0)
           == lax.broadcasted_iota(jnp.int32, (c_len, c_len), 1))
    zeros_rows = jnp.zeros((c_len, hd), F32)
    n_sq = c_len.bit_length() - 1

    units = [(cc, h) for cc in range(ts // c_len) for h in range(nh)]

    def cut(ref, u):
        cc, h = u
        return ref[h, cc * c_len:(cc + 1) * c_len, :]

    at_c = [cut(at_ref, u) for u in units]
    rt_c = [cut(rt_ref, u) for u in units]
    bh_c = [cut(bh_ref, u) for u in units]
    kh_c = [cut(kh_ref, u) for u in units]
    g_c = [jnp.exp(cum_ref[h, (cc + 1) * c_len - 1:(cc + 1) * c_len, :]) for cc, h in units]
    p_all = [_mm(jnp.concatenate([a, r_], axis=0), jnp.concatenate([b_, k_], axis=0), NT)
             for a, r_, b_, k_ in zip(at_c, rt_c, bh_c, kh_c)]
    top = [jnp.where(strict, pa[0:c_len], 0.0) for pa in p_all]
    bot = [jnp.where(incl, pa[c_len:], 0.0) for pa in p_all]
    x2 = [_mm(t, jnp.concatenate([zeros_rows, cut(v_ref, u)], axis=0)) for t, u in zip(top, units)]
    x = [jnp.concatenate([a, b_], axis=1) for a, b_ in zip(at_c, x2)]
    p = [t[:, 0:c_len] for t in top]
    for it in range(n_sq):
        if it + 1 < n_sq:
            px = [_mm(pi, jnp.concatenate([xi, pi], axis=1)) for pi, xi in zip(p, x)]
            x = [xi + r_[:, 0:2 * hd] for xi, r_ in zip(x, px)]
            p = [r_[:, 2 * hd:] for r_ in px]
        else:
            x = [xi + _mm(pi, xi) for pi, xi in zip(p, x)]
    rhs = [jnp.concatenate([xi, cut(vr_ref, u)], axis=0) for xi, u in zip(x, units)]
    res_y = [_mm(b_, r_) for b_, r_ in zip(bot, rhs)]
    res_m = [_mm(jnp.concatenate([b_ * g, k_ * g], axis=0), r_, TN)
             for b_, k_, g, r_ in zip(bh_c, kh_c, g_c, rhs)]
    r_eff = [r_ + ry[:, 0:hd] for r_, ry in zip(rt_c, res_y)]
    g_mat = [jnp.where(eye, g, 0.0) + rm[:, 0:hd] for g, rm in zip(g_c, res_m)]
    state = [state_ref[h] for h in range(nh)]
    for i, (cc, h) in enumerate(units):
        prod = _mm(jnp.concatenate([r_eff[i], g_mat[i]], axis=0), state[h])
        y_ref[h, cc * c_len:(cc + 1) * c_len, :] = prod[0:c_len] + res_y[i][:, hd:]
        state[h] = prod[c_len:] + res_m[i][:, hd:]
    for h in range(nh):
        state_ref[h] = state[h]

    y = jnp.concatenate([y_ref[h] for h in range(nh)], axis=-1)
    inv_hd = 1.0 / hd
    mean = _mm(y, hsum) * inv_hd
    yc = y - mean
    var = _mm(yc * yc, hsum) * inv_hd
    yn = yc * lax.rsqrt(var + RWKV_LNX_EPS) * lnx_g + lnx_b
    bonus = _mm(r * k2 * r_k, hsum) * v
    o_ref[0] = ((yn + bonus) * gate).astype(o_ref.dtype)


def _rwkv(proj, mu, w0, w2, a0, a2, g2, k_k, k_a, r_k, lnx_g, lnx_b, ts=512):
    b, s, _ = proj.shape
    wlo = jnp.zeros((LANE, 2 * D_RWKV), F32)
    wlo = wlo.at[0:DECAY_LORA, 0:D_RWKV].set(w2).at[DECAY_LORA:LANE, D_RWKV:].set(a2)
    w0a0 = jnp.concatenate([w0, a0]).reshape(1, 2 * D_RWKV)
    vecs = jnp.zeros((8, D_RWKV), F32)
    vecs = vecs.at[0].set(k_k).at[1].set(k_a).at[2].set(r_k.reshape(-1)).at[3].set(lnx_g).at[4].set(lnx_b)
    tri_rows = min(ts, 256)
    ti = jnp.arange(tri_rows)
    tri = ((ti[:, None] >= ti[None, :]) & (ti[:, None] // RWKV_CHUNK == ti[None, :] // RWKV_CHUNK)).astype(BF16)
    hi_ = jnp.arange(D_RWKV) // HEAD_DIM
    hsum = (hi_[:, None] == hi_[None, :]).astype(BF16)
    const = lambda shape: pl.BlockSpec(shape, lambda i, j: (0,) * len(shape))
    head_buf = pltpu.VMEM((RWKV_HEADS, ts, HEAD_DIM), F32)
    return pl.pallas_call(
        functools.partial(_rwkv_kernel, ts=ts),
        grid=(b, s // ts),
        in_specs=[
            pl.BlockSpec((1, ts, A_COLS), lambda i, j: (i, j, 0)),
            const((1, A_COLS)),
            const((LANE, 2 * D_RWKV)),
            const((1, 2 * D_RWKV)),
            const((GATE_LORA, D_RWKV)),
            const((8, D_RWKV)),
            const((tri_rows, tri_rows)),
            const((D_RWKV, D_RWKV)),
        ],
        out_specs=pl.BlockSpec((1, ts, D_RWKV), lambda i, j: (i, j, 0)),
        out_shape=jax.ShapeDtypeStruct((b, s, D_RWKV), BF16),
        scratch_shapes=[
            pltpu.VMEM((RWKV_HEADS, HEAD_DIM, HEAD_DIM), F32),
            pltpu.VMEM((8, A_COLS), F32),
            head_buf, head_buf, head_buf, head_buf, head_buf, head_buf, head_buf,
            pltpu.VMEM((RWKV_HEADS, ts, 2 * HEAD_DIM), F32),
        ],
        compiler_params=_params("parallel", "arbitrary"),
        name="rwkv7",
    )(proj, mu.reshape(1, A_COLS), wlo, w0a0, g2, vecs, tri, hsum)


def _sublane_allreduce(x, op):
    for shift in (4, 2, 1):
        x = op(x, pltpu.roll(x, shift, axis=0))
    return x


def _moba_kernel(q_ref, k_ref, v_ref, o_ref, *, s):
    blk = MOBA_BLOCK
    nb = s // blk
    sub = 8
    q = q_ref[0]
    k = k_ref[0]
    k16 = k.astype(BF16)
    vt16 = v_ref[0].T.astype(BF16)
    lane = lax.broadcasted_iota(jnp.int32, (1, LANE), 1)
    kbar = jnp.concatenate(
        [jnp.mean(k[n * blk:(n + 1) * blk], axis=0, keepdims=True) for n in range(nb)], axis=0)
    key_i = lax.broadcasted_iota(jnp.int32, (blk, blk), 0)
    qry_i = lax.broadcasted_iota(jnp.int32, (blk, blk), 1)
    neg_inf = -jnp.inf
    causal_pen = jnp.where(key_i <= qry_i, 0.0, neg_inf).reshape(blk // sub, sub, blk)
    blk_id = lax.broadcasted_iota(jnp.int32, (nb, blk), 0)
    out_row = lax.broadcasted_iota(jnp.int32, (LANE, blk), 0)
    n_top = min(MOBA_TOPK, nb - 1)

    def logits3(n, qh16):
        lg = _dot(k16[n * blk:(n + 1) * blk], qh16, NT)
        return lg.reshape(blk // sub, sub, blk)

    for qi in range(nb):
        rows = slice(qi * blk, (qi + 1) * blk)
        outs = []
        for hh in range(2):
            head_lanes = (lane >= hh * HEAD_DIM) & (lane < (hh + 1) * HEAD_DIM)
            qh = jnp.where(head_lanes, q[rows], 0.0)
            qh16 = (qh * (HEAD_DIM ** -0.5)).astype(BF16)
            logits = []
            if qi > 0 and n_top > 0:
                sc = _dot(kbar, qh, NT, HI)
                cnt = jnp.zeros((nb, blk), F32)
                for m in range(qi):
                    cm = sc[m:m + 1, :]
                    beats = (cm > sc) | ((cm == sc) & (blk_id > m))
                    cnt = cnt + jnp.where(beats, 1.0, 0.0)
                pen = jnp.where((cnt < n_top) & (blk_id < qi), 0.0, neg_inf)
                for n in range(qi):
                    pen_n = jnp.broadcast_to(pen[n:n + 1, :], (sub, blk))
                    logits.append(logits3(n, qh16) + pen_n[None])
            logits.append(logits3(qi, qh16) + causal_pen)
            m_rep = functools.reduce(jnp.maximum, [jnp.max(lg, axis=0) for lg in logits])
            m_rep = _sublane_allreduce(m_rep, jnp.maximum)
            l_rep = jnp.zeros((sub, blk), F32)
            acc = jnp.zeros((LANE, blk), F32)
            first = qi + 1 - len(logits)
            for n, lg in enumerate(logits):
                p = jnp.exp(lg - m_rep[None])
                l_rep = l_rep + jnp.sum(p, axis=0)
                kv = first + n
                acc = acc + _dot(vt16[:, kv * blk:(kv + 1) * blk], p.reshape(blk, blk).astype(BF16))
            l_rep = _sublane_allreduce(l_rep, jnp.add)
            outs.append((acc.reshape(LANE // sub, sub, blk) / l_rep[None]).reshape(LANE, blk))
        out_t = jnp.where(out_row < HEAD_DIM, outs[0], outs[1])
        o_ref[0, rows, :] = out_t.T.astype(o_ref.dtype)


def _moba(proj):
    b, s, _ = proj.shape
    n_pair = D_MOBA // LANE
    base = A_COLS // LANE
    spec = lambda off: pl.BlockSpec((1, s, LANE), lambda i, p: (i, 0, base + off + p))
    return pl.pallas_call(
        functools.partial(_moba_kernel, s=s),
        grid=(b, n_pair),
        in_specs=[spec(0), spec(n_pair), spec(2 * n_pair)],
        out_specs=pl.BlockSpec((1, s, LANE), lambda i, p: (i, 0, p)),
        out_shape=jax.ShapeDtypeStruct((b, s, D_MOBA), BF16),
        compiler_params=_params("parallel", "parallel"),
        name="moba",
    )(proj, proj, proj)


def _gelu(x):
    return 0.5 * x * (1.0 + jnp.tanh(0.7978845608028654 * (x + 0.044715 * (x * x * x))))


def _gmlp_kernel(u_ref, v_ref, lng_ref, lnb_ref, ws_ref, bias_ref, o_ref, *, ts):
    ch = GMLP_CHUNK
    u = _gelu(u_ref[0])
    v = _gelu(v_ref[0])
    mean = jnp.mean(v, axis=-1, keepdims=True)
    vc = v - mean
    var = jnp.mean(vc * vc, axis=-1, keepdims=True)
    vn = ((vc * lax.rsqrt(var + NORM_EPS)) * lng_ref[...] + lnb_ref[...]).astype(BF16)
    ri = lax.broadcasted_iota(jnp.int32, (ch, ch), 0)
    ci = lax.broadcasted_iota(jnp.int32, (ch, ch), 1)
    tril = ri >= ci
    w = [jnp.where(tril, ws_ref[g], 0.0).astype(BF16) for g in range(GMLP_GROUPS)]
    lane = lax.broadcasted_iota(jnp.int32, (1, D_GMLP), 1)
    bias = bias_ref[...]
    for c in range(ts // ch):
        rows = slice(c * ch, (c + 1) * ch)
        vchunk = vn[rows]
        mixed = _dot(w[0], vchunk)
        for g in range(1, GMLP_GROUPS):
            mixed = jnp.where(lane >= g * HEAD_DIM, _dot(w[g], vchunk), mixed)
        o_ref[0, rows, :] = (u[rows] * (mixed + bias)).astype(o_ref.dtype)


def _gmlp(proj, ln_g, ln_b, w_s, b_s, ts=512):
    b, s, _ = proj.shape
    base = (A_COLS + B_COLS) // D_GMLP
    bias = jnp.repeat(b_s.T, HEAD_DIM, axis=1)
    const = lambda shape: pl.BlockSpec(shape, lambda i, j: (0,) * len(shape))
    return pl.pallas_call(
        functools.partial(_gmlp_kernel, ts=ts),
        grid=(b, s // ts),
        in_specs=[
            pl.BlockSpec((1, ts, D_GMLP), lambda i, j: (i, j, base)),
            pl.BlockSpec((1, ts, D_GMLP), lambda i, j: (i, j, base + 1)),
            const((1, D_GMLP)),
            const((1, D_GMLP)),
            const((GMLP_GROUPS, GMLP_CHUNK, GMLP_CHUNK)),
            const((GMLP_CHUNK, D_GMLP)),
        ],
        out_specs=pl.BlockSpec((1, ts, D_GMLP), lambda i, j: (i, j, 0)),
        out_shape=jax.ShapeDtypeStruct((b, s, D_GMLP), BF16),
        compiler_params=_params("parallel", "parallel"),
        name="gmlp",
    )(proj, proj, ln_g.reshape(1, D_GMLP), ln_b.reshape(1, D_GMLP), w_s, bias)


def _out_proj_kernel(a_ref, b_ref, c_ref, wa_ref, wb_ref, wc_ref, x_ref, g_ref, o_ref):
    mix = _dot(a_ref[...], wa_ref[...]) + _dot(b_ref[...], wb_ref[...]) + _dot(c_ref[...], wc_ref[...])
    o_ref[...] = x_ref[...] + _rms(mix, g_ref[...])


def _out_proj(a2, b2, c2, w_bf16, x2, g, tm=512):
    m, d = x2.shape
    wa = w_bf16[0:D_RWKV]
    wb = w_bf16[D_RWKV:D_RWKV + D_MOBA]
    wc = w_bf16[D_RWKV + D_MOBA:]
    row = lambda n: pl.BlockSpec((tm, n), lambda i: (i, 0))
    const = lambda shape: pl.BlockSpec(shape, lambda i: (0,) * len(shape))
    return pl.pallas_call(
        _out_proj_kernel,
        grid=(m // tm,),
        in_specs=[row(D_RWKV), row(D_MOBA), row(D_GMLP), const(wa.shape), const(wb.shape), const(wc.shape),
                  row(d), const((1, d))],
        out_specs=row(d),
        out_shape=jax.ShapeDtypeStruct((m, d), F32),
        compiler_params=_params("parallel"),
        name="out_proj",
    )(a2, b2, c2, wa, wb, wc, x2, g.reshape(1, d))


def _ffn_kernel(x_ref, gpre_ref, wi_ref, wo_ref, gpost_ref, o_ref, *, tf):
    x = x_ref[...]
    h = _rms(x, gpre_ref[...]).astype(BF16)
    f = None
    for c in range(D_FF // tf):
        gate = _dot(h, wi_ref[:, c * tf:(c + 1) * tf])
        up = _dot(h, wi_ref[:, D_FF + c * tf:D_FF + (c + 1) * tf])
        act = (gate * _sigmoid(gate) * up).astype(BF16)
        part = _dot(act, wo_ref[c * tf:(c + 1) * tf, :])
        f = part if f is None else f + part
    o_ref[...] = x + _rms(f, gpost_ref[...])


def _ffn(x2, g_pre, w_in_bf16, w_out_bf16, g_post, tm=512, tf=256):
    m, d = x2.shape
    resident = lambda shape: pl.BlockSpec(shape, lambda i: (0, 0), pipeline_mode=pl.Buffered(1))
    return pl.pallas_call(
        functools.partial(_ffn_kernel, tf=tf),
        grid=(m // tm,),
        in_specs=[
            pl.BlockSpec((tm, d), lambda i: (i, 0)),
            resident((1, d)),
            resident((d, 2 * D_FF)),
            resident((D_FF, d)),
            resident((1, d)),
        ],
        out_specs=pl.BlockSpec((tm, d), lambda i: (i, 0)),
        out_shape=jax.ShapeDtypeStruct((m, d), F32),
        compiler_params=_params("parallel"),
        name="ffn",
    )(x2, g_pre.reshape(1, d), w_in_bf16, w_out_bf16, g_post.reshape(1, d))


def kernel(x, pre_mix_g, w_in, rwkv_mu, rwkv_w0, rwkv_w2, rwkv_a0, rwkv_a2, rwkv_g2, rwkv_k_k, rwkv_k_a, rwkv_r_k, rwkv_lnx_g, rwkv_lnx_b, gmlp_ln_g, gmlp_ln_b, gmlp_w_s, gmlp_b_s, w_out, post_mix_g, pre_ffn_g, w_ffn_in, w_ffn_out, post_ffn_g):
    b, s, d = x.shape
    m = b * s
    x2 = x.reshape(m, d)
    for l in range(w_in.shape[0]):
        proj = _in_proj(x2, pre_mix_g[l], w_in[l].astype(BF16)).reshape(b, s, D_IN)
        a_out = _rwkv(proj, rwkv_mu[l], rwkv_w0[l], rwkv_w2[l], rwkv_a0[l], rwkv_a2[l], rwkv_g2[l],
                      rwkv_k_k[l], rwkv_k_a[l], rwkv_r_k[l], rwkv_lnx_g[l], rwkv_lnx_b[l])
        b_out = _moba(proj)
        c_out = _gmlp(proj, gmlp_ln_g[l], gmlp_ln_b[l], gmlp_w_s[l], gmlp_b_s[l])
        x2 = _out_proj(a_out.reshape(m, D_RWKV), b_out.reshape(m, D_MOBA), c_out.reshape(m, D_GMLP),
                       w_out[l].astype(BF16), x2, post_mix_g[l])
        x2 = _ffn(x2, pre_ffn_g[l], w_ffn_in[l].astype(BF16), w_ffn_out[l].astype(BF16), post_ffn_g[l])
    return x2.reshape(b, s, d)
```

```python
import functools

import jax
import jax.numpy as jnp
from jax import lax
from jax.experimental import pallas as pl
from jax.experimental.pallas import tpu as pltpu

F32 = jnp.float32
BF16 = jnp.bfloat16

D_MODEL = 1024
HEAD_DIM = 64
RWKV_HEADS = 6
D_RWKV = RWKV_HEADS * HEAD_DIM
DECAY_LORA = 64
AAA_LORA = 64
GATE_LORA = 128
RWKV_LNX_EPS = 64e-5
MOBA_HEADS = 6
D_MOBA = MOBA_HEADS * HEAD_DIM
MOBA_BLOCK = 256
MOBA_TOPK = 3
GMLP_GROUPS = 4
D_GMLP = GMLP_GROUPS * HEAD_DIM
GMLP_CHUNK = 128
D_FF = 2816
NORM_EPS = 1e-6
LOG2_E = 1.4426950408889634
A_COLS = 3 * D_RWKV + DECAY_LORA + AAA_LORA + GATE_LORA
B_COLS = 3 * D_MOBA
C_COLS = 2 * D_GMLP
D_IN = A_COLS + B_COLS + C_COLS

LANE = 128
RWKV_CHUNK = 64
VMEM_LIMIT = 56 * 1024 * 1024

NN = (((1,), (0,)), ((), ()))
NT = (((1,), (1,)), ((), ()))
TN = (((0,), (0,)), ((), ()))
HI = lax.Precision.HIGHEST


def _dot(a, b, dims=NN, precision=None):
    return lax.dot_general(a, b, dims, precision=precision, preferred_element_type=F32)


def _mm(a, b, dims=NN):
    return _dot(a.astype(BF16), b.astype(BF16), dims)


def _split2(x):
    hi = x.astype(BF16)
    lo = (x - hi.astype(F32)).astype(BF16)
    return hi, lo


def _dot_exact_lhs(w_bf16, x):
    hi, lo = _split2(x)
    return _dot(w_bf16, hi) + _dot(w_bf16, lo)


def _sigmoid(x):
    return 1.0 / (1.0 + jnp.exp(-x))


def _rms(x, g):
    return x * lax.rsqrt(jnp.mean(x * x, axis=-1, keepdims=True) + NORM_EPS) * g


def _params(*sem):
    return pltpu.CompilerParams(dimension_semantics=sem, vmem_limit_bytes=VMEM_LIMIT)


def _in_proj_kernel(x_ref, g_ref, w_ref, o_ref, *, tn):
    h = _rms(x_ref[...], g_ref[...]).astype(BF16)
    for c in range(w_ref.shape[1] // tn):
        o_ref[:, c * tn:(c + 1) * tn] = _dot(h, w_ref[:, c * tn:(c + 1) * tn])


def _in_proj(x2, g, w_bf16, tm=512, tn=512):
    m, d = x2.shape
    n = w_bf16.shape[1]
    return pl.pallas_call(
        functools.partial(_in_proj_kernel, tn=tn),
        grid=(m // tm,),
        in_specs=[
            pl.BlockSpec((tm, d), lambda i: (i, 0)),
            pl.BlockSpec((1, d), lambda i: (0, 0), pipeline_mode=pl.Buffered(1)),
            pl.BlockSpec((d, n), lambda i: (0, 0), pipeline_mode=pl.Buffered(1)),
        ],
        out_specs=pl.BlockSpec((tm, n), lambda i: (i, 0)),
        out_shape=jax.ShapeDtypeStruct((m, n), F32),
        compiler_params=_params("parallel"),
        name="in_proj",
    )(x2, g.reshape(1, d), w_bf16)


def _rwkv_kernel(ya_ref, mu_ref, wlo_ref, w0a0_ref, g2_ref, vecs_ref, tri_ref, hsum_ref, o_ref,
                 state_ref, prev_ref, at_ref, rt_ref, bh_ref, kh_ref, v_ref, cum_ref, y_ref, vr_ref, *, ts):
    nh, hd, c_len = RWKV_HEADS, HEAD_DIM, RWKV_CHUNK

    @pl.when(pl.program_id(1) == 0)
    def _():
        state_ref[...] = jnp.zeros_like(state_ref)
        prev_ref[...] = jnp.zeros_like(prev_ref)

    ya = ya_ref[0]
    row = lax.broadcasted_iota(jnp.int32, ya.shape, 0)
    prev = jnp.where(row == 0, prev_ref[0:1, :], pltpu.roll(ya, 1, axis=0))
    prev_ref[0:1, :] = ya[ts - 1:ts, :]
    xs = ya + mu_ref[...] * (prev - ya)

    r = xs[:, 0:D_RWKV]
    k = xs[:, D_RWKV:2 * D_RWKV]
    v = xs[:, 2 * D_RWKV:3 * D_RWKV]
    wa = xs[:, 3 * D_RWKV:3 * D_RWKV + LANE]
    gd = xs[:, 3 * D_RWKV + LANE:A_COLS]
    lane = lax.broadcasted_iota(jnp.int32, wa.shape, 1)
    wa = jnp.where(lane < DECAY_LORA, jnp.tanh(wa), wa)
    lo = _mm(wa, wlo_ref[...]) + w0a0_ref[...]
    z = -lo[:, 0:D_RWKV]
    softplus = jnp.maximum(z, 0.0) + jnp.log(1.0 + jnp.exp(-jnp.abs(z)))
    logdecay = -jnp.exp(-softplus - 0.5)
    alpha = _sigmoid(lo[:, D_RWKV:2 * D_RWKV])
    gate = _mm(_sigmoid(gd), g2_ref[...])

    k_k = vecs_ref[0:1, :]
    k_a = vecs_ref[1:2, :]
    r_k = vecs_ref[2:3, :]
    lnx_g = vecs_ref[3:4, :]
    lnx_b = vecs_ref[4:5, :]
    hsum = hsum_ref[...]

    kk = k * k_k
    kk = kk / jnp.maximum(jnp.sqrt(_mm(kk * kk, hsum)), 1e-12)
    k2 = k * (1.0 + (alpha - 1.0) * k_a)
    tri = tri_ref[...]
    tr = tri.shape[0]
    cum = jnp.concatenate([_dot_exact_lhs(tri, logdecay[i * tr:(i + 1) * tr]) for i in range(ts // tr)],
                          axis=0)
    e_neg = jnp.exp(-cum)
    at = -kk * jnp.exp(cum - logdecay)
    rt = r * jnp.exp(cum)
    bh = kk * alpha * e_neg
    kh = k2 * e_neg
    zeros_half = jnp.zeros((ts, hd), F32)
    for h in range(nh):
        sl = slice(h * hd, (h + 1) * hd)
        at_ref[h] = at[:, sl]
        rt_ref[h] = rt[:, sl]
        bh_ref[h] = bh[:, sl]
        kh_ref[h] = kh[:, sl]
        v_ref[h] = v[:, sl]
        vr_ref[h] = jnp.concatenate([zeros_half, v[:, sl]], axis=1)
        cum_ref[h] = cum[:, sl]

    ri = lax.broadcasted_iota(jnp.int32, (c_len, 2 * c_len), 0)
    ci = lax.broadcasted_iota(jnp.int32, (c_len, 2 * c_len), 1)
    ci = jnp.where(ci >= c_len, ci - c_len, ci)
    strict = ri > ci
    incl = ri >= ci
    eye = (lax.broadcasted_iota(jnp.int32, (c_len, c_len), 0)
           == lax.broadcasted_iota(jnp.int32, (c_len, c_len), 1))
    zeros_rows = jnp.zeros((c_len, hd), F32)
    n_sq = c_len.bit_length() - 1

    units = [(cc, h) for cc in range(ts // c_len) for h in range(nh)]

    def cut(ref, u):
        cc, h = u
        return ref[h, cc * c_len:(cc + 1) * c_len, :]

    at_c = [cut(at_ref, u) for u in units]
    rt_c = [cut(rt_ref, u) for u in units]
    bh_c = [cut(bh_ref, u) for u in units]
    kh_c = [cut(kh_ref, u) for u in units]
    g_c = [jnp.exp(cum_ref[h, (cc + 1) * c_len - 1:(cc + 1) * c_len, :]) for cc, h in units]
    p_all = [_mm(jnp.concatenate([a, r_], axis=0), jnp.concatenate([b_, k_], axis=0), NT)
             for a, r_, b_, k_ in zip(at_c, rt_c, bh_c, kh_c)]
    top = [jnp.where(strict, pa[0:c_len], 0.0) for pa in p_all]
    bot = [jnp.where(incl, pa[c_len:], 0.0) for pa in p_all]
    x2 = [_mm(t, jnp.concatenate([zeros_rows, cut(v_ref, u)], axis=0)) for t, u in zip(top, units)]
    x = [jnp.concatenate([a, b_], axis=1) for a, b_ in zip(at_c, x2)]
    p = [t[:, 0:c_len] for t in top]
    for it in range(n_sq):
        if it + 1 < n_sq:
            px = [_mm(pi, jnp.concatenate([xi, pi], axis=1)) for pi, xi in zip(p, x)]
            x = [xi + r_[:, 0:2 * hd] for xi, r_ in zip(x, px)]
            p = [r_[:, 2 * hd:] for r_ in px]
        else:
            x = [xi + _mm(pi, xi) for pi, xi in zip(p, x)]
    rhs = [jnp.concatenate([xi, cut(vr_ref, u)], axis=0) for xi, u in zip(x, units)]
    res_y = [_mm(b_, r_) for b_, r_ in zip(bot, rhs)]
    res_m = [_mm(jnp.concatenate([b_ * g, k_ * g], axis=0), r_, TN)
             for b_, k_, g, r_ in zip(bh_c, kh_c, g_c, rhs)]
    r_eff = [r_ + ry[:, 0:hd] for r_, ry in zip(rt_c, res_y)]
    g_mat = [jnp.where(eye, g, 0.0) + rm[:, 0:hd] for g, rm in zip(g_c, res_m)]
    state = [state_ref[h] for h in range(nh)]
    for i, (cc, h) in enumerate(units):
        prod = _mm(jnp.concatenate([r_eff[i], g_mat[i]], axis=0), state[h])
        y_ref[h, cc * c_len:(cc + 1) * c_len, :] = prod[0:c_len] + res_y[i][:, hd:]
        state[h] = prod[c_len:] + res_m[i][:, hd:]
    for h in range(nh):
        state_ref[h] = state[h]

    y = jnp.concatenate([y_ref[h] for h in range(nh)], axis=-1)
    inv_hd = 1.0 / hd
    mean = _mm(y, hsum) * inv_hd
    yc = y - mean
    var = _mm(yc * yc, hsum) * inv_hd
    yn = yc * lax.rsqrt(var + RWKV_LNX_EPS) * lnx_g + lnx_b
    bonus = _mm(r * k2 * r_k, hsum) * v
    o_ref[0] = ((yn + bonus) * gate).astype(o_ref.dtype)


def _rwkv(proj, mu, w0, w2, a0, a2, g2, k_k, k_a, r_k, lnx_g, lnx_b, ts=512):
    b, s, _ = proj.shape
    wlo = jnp.zeros((LANE, 2 * D_RWKV), F32)
    wlo = wlo.at[0:DECAY_LORA, 0:D_RWKV].set(w2).at[DECAY_LORA:LANE, D_RWKV:].set(a2)
    w0a0 = jnp.concatenate([w0, a0]).reshape(1, 2 * D_RWKV)
    vecs = jnp.zeros((8, D_RWKV), F32)
    vecs = vecs.at[0].set(k_k).at[1].set(k_a).at[2].set(r_k.reshape(-1)).at[3].set(lnx_g).at[4].set(lnx_b)
    tri_rows = min(ts, 256)
    ti = jnp.arange(tri_rows)
    tri = ((ti[:, None] >= ti[None, :]) & (ti[:, None] // RWKV_CHUNK == ti[None, :] // RWKV_CHUNK)).astype(BF16)
    hi_ = jnp.arange(D_RWKV) // HEAD_DIM
    hsum = (hi_[:, None] == hi_[None, :]).astype(BF16)
    const = lambda shape: pl.BlockSpec(shape, lambda i, j: (0,) * len(shape))
    head_buf = pltpu.VMEM((RWKV_HEADS, ts, HEAD_DIM), F32)
    return pl.pallas_call(
        functools.partial(_rwkv_kernel, ts=ts),
        grid=(b, s // ts),
        in_specs=[
            pl.BlockSpec((1, ts, A_COLS), lambda i, j: (i, j, 0)),
            const((1, A_COLS)),
            const((LANE, 2 * D_RWKV)),
            const((1, 2 * D_RWKV)),
            const((GATE_LORA, D_RWKV)),
            const((8, D_RWKV)),
            const((tri_rows, tri_rows)),
            const((D_RWKV, D_RWKV)),
        ],
        out_specs=pl.BlockSpec((1, ts, D_RWKV), lambda i, j: (i, j, 0)),
        out_shape=jax.ShapeDtypeStruct((b, s, D_RWKV), BF16),
        scratch_shapes=[
            pltpu.VMEM((RWKV_HEADS, HEAD_DIM, HEAD_DIM), F32),
            pltpu.VMEM((8, A_COLS), F32),
            head_buf, head_buf, head_buf, head_buf, head_buf, head_buf, head_buf,
            pltpu.VMEM((RWKV_HEADS, ts, 2 * HEAD_DIM), F32),
        ],
        compiler_params=_params("parallel", "arbitrary"),
        name="rwkv7",
    )(proj, mu.reshape(1, A_COLS), wlo, w0a0, g2, vecs, tri, hsum)


def _sublane_allreduce(x, op):
    for shift in (4, 2, 1):
        x = op(x, pltpu.roll(x, shift, axis=0))
    return x


def _moba_kernel(q_ref, k_ref, v_ref, o_ref, *, s):
    blk = MOBA_BLOCK
    nb = s // blk
    sub = 8
    q = q_ref[0]
    k = k_ref[0]
    k16 = k.astype(BF16)
    vt16 = v_ref[0].T.astype(BF16)
    lane = lax.broadcasted_iota(jnp.int32, (1, LANE), 1)
    kbar = jnp.concatenate(
        [jnp.mean(k[n * blk:(n + 1) * blk], axis=0, keepdims=True) for n in range(nb)], axis=0)
    key_i = lax.broadcasted_iota(jnp.int32, (blk, blk), 0)
    qry_i = lax.broadcasted_iota(jnp.int32, (blk, blk), 1)
    neg_inf = -jnp.inf
    causal_pen = jnp.where(key_i <= qry_i, 0.0, neg_inf).reshape(blk // sub, sub, blk)
    blk_id = lax.broadcasted_iota(jnp.int32, (nb, blk), 0)
    out_row = lax.broadcasted_iota(jnp.int32, (LANE, blk), 0)
    n_top = min(MOBA_TOPK, nb - 1)

    units = [(qi, hh) for qi in range(nb) for hh in range(2)]
    logits = []
    for qi, hh in units:
        rows = slice(qi * blk, (qi + 1) * blk)
        head_lanes = (lane >= hh * HEAD_DIM) & (lane < (hh + 1) * HEAD_DIM)
        qh = jnp.where(head_lanes, q[rows], 0.0)
        qh16 = (qh * (HEAD_DIM ** -0.5 * LOG2_E)).astype(BF16)
        first = 0 if n_top > 0 else qi
        kq = lambda n: _dot(k16[n * blk:(n + 1) * blk], qh16, NT).reshape(blk // sub, sub, blk)
        lg3 = []
        if qi > first:
            sc = _dot(kbar, qh, NT, HI)
            cnt = jnp.zeros((nb, blk), F32)
            for m in range(qi):
                cm = sc[m:m + 1, :]
                beats = (cm > sc) | ((cm == sc) & (blk_id > m))
                cnt = cnt + jnp.where(beats, 1.0, 0.0)
            pen = jnp.where((cnt < n_top) & (blk_id < qi), 0.0, neg_inf)
            for n in range(qi):
                lg3.append(kq(n) + jnp.broadcast_to(pen[n:n + 1, :], (sub, blk))[None])
        lg3.append(kq(qi) + causal_pen)
        logits.append((first, lg3))
    m_rep = [_sublane_allreduce(functools.reduce(jnp.maximum, [jnp.max(lg, axis=0) for lg in lgs]),
                                jnp.maximum) for _, lgs in logits]
    outs = []
    for (first, lgs), m_u in zip(logits, m_rep):
        l_rep = jnp.zeros((sub, blk), F32)
        acc = jnp.zeros((LANE, blk), F32)
        for i, lg in enumerate(lgs):
            p = jnp.exp2(lg - m_u[None])
            l_rep = l_rep + jnp.sum(p, axis=0)
            kv = first + i
            acc = acc + _dot(vt16[:, kv * blk:(kv + 1) * blk], p.reshape(blk, blk).astype(BF16))
        l_rep = _sublane_allreduce(l_rep, jnp.add)
        outs.append((acc.reshape(LANE // sub, sub, blk) / l_rep[None]).reshape(LANE, blk))
    for qi in range(nb):
        out_t = jnp.where(out_row < HEAD_DIM, outs[2 * qi], outs[2 * qi + 1])
        o_ref[0, qi * blk:(qi + 1) * blk, :] = out_t.T.astype(o_ref.dtype)


def _moba(proj):
    b, s, _ = proj.shape
    n_pair = D_MOBA // LANE
    base = A_COLS // LANE
    spec = lambda off: pl.BlockSpec((1, s, LANE), lambda i, p: (i, 0, base + off + p))
    return pl.pallas_call(
        functools.partial(_moba_kernel, s=s),
        grid=(b, n_pair),
        in_specs=[spec(0), spec(n_pair), spec(2 * n_pair)],
        out_specs=pl.BlockSpec((1, s, LANE), lambda i, p: (i, 0, p)),
        out_shape=jax.ShapeDtypeStruct((b, s, D_MOBA), BF16),
        compiler_params=_params("parallel", "parallel"),
        name="moba",
    )(proj, proj, proj)


def _gelu(x):
    return 0.5 * x * (1.0 + jnp.tanh(0.7978845608028654 * (x + 0.044715 * (x * x * x))))


def _gmlp_kernel(u_ref, v_ref, lng_ref, lnb_ref, ws_ref, bias_ref, o_ref, *, ts):
    ch = GMLP_CHUNK
    u = _gelu(u_ref[0])
    v = _gelu(v_ref[0])
    mean = jnp.mean(v, axis=-1, keepdims=True)
    vc = v - mean
    var = jnp.mean(vc * vc, axis=-1, keepdims=True)
    vn = ((vc * lax.rsqrt(var + NORM_EPS)) * lng_ref[...] + lnb_ref[...]).astype(BF16)
    ri = lax.broadcasted_iota(jnp.int32, (ch, ch), 0)
    ci = lax.broadcasted_iota(jnp.int32, (ch, ch), 1)
    tril = ri >= ci
    w = [jnp.where(tril, ws_ref[g], 0.0).astype(BF16) for g in range(GMLP_GROUPS)]
    lane = lax.broadcasted_iota(jnp.int32, (1, D_GMLP), 1)
    bias = bias_ref[...]
    for c in range(ts // ch):
        rows = slice(c * ch, (c + 1) * ch)
        vchunk = vn[rows]
        mixed = _dot(w[0], vchunk)
        for g in range(1, GMLP_GROUPS):
            mixed = jnp.where(lane >= g * HEAD_DIM, _dot(w[g], vchunk), mixed)
        o_ref[0, rows, :] = (u[rows] * (mixed + bias)).astype(o_ref.dtype)


def _gmlp(proj, ln_g, ln_b, w_s, b_s, ts=512):
    b, s, _ = proj.shape
    base = (A_COLS + B_COLS) // D_GMLP
    bias = jnp.repeat(b_s.T, HEAD_DIM, axis=1)
    const = lambda shape: pl.BlockSpec(shape, lambda i, j: (0,) * len(shape))
    return pl.pallas_call(
        functools.partial(_gmlp_kernel, ts=ts),
        grid=(b, s // ts),
        in_specs=[
            pl.BlockSpec((1, ts, D_GMLP), lambda i, j: (i, j, base)),
            pl.BlockSpec((1, ts, D_GMLP), lambda i, j: (i, j, base + 1)),
            const((1, D_GMLP)),
            const((1, D_GMLP)),
            const((GMLP_GROUPS, GMLP_CHUNK, GMLP_CHUNK)),
            const((GMLP_CHUNK, D_GMLP)),
        ],
        out_specs=pl.BlockSpec((1, ts, D_GMLP), lambda i, j: (i, j, 0)),
        out_shape=jax.ShapeDtypeStruct((b, s, D_GMLP), BF16),
        compiler_params=_params("parallel", "parallel"),
        name="gmlp",
    )(proj, proj, ln_g.reshape(1, D_GMLP), ln_b.reshape(1, D_GMLP), w_s, bias)


def _out_proj_kernel(a_ref, b_ref, c_ref, wa_ref, wb_ref, wc_ref, x_ref, g_ref, o_ref):
    mix = _dot(a_ref[...], wa_ref[...]) + _dot(b_ref[...], wb_ref[...]) + _dot(c_ref[...], wc_ref[...])
    o_ref[...] = x_ref[...] + _rms(mix, g_ref[...])


def _out_proj(a2, b2, c2, w_bf16, x2, g, tm=512):
    m, d = x2.shape
    wa = w_bf16[0:D_RWKV]
    wb = w_bf16[D_RWKV:D_RWKV + D_MOBA]
    wc = w_bf16[D_RWKV + D_MOBA:]
    row = lambda n: pl.BlockSpec((tm, n), lambda i: (i, 0))
    const = lambda shape: pl.BlockSpec(shape, lambda i: (0,) * len(shape))
    return pl.pallas_call(
        _out_proj_kernel,
        grid=(m // tm,),
        in_specs=[row(D_RWKV), row(D_MOBA), row(D_GMLP), const(wa.shape), const(wb.shape), const(wc.shape),
                  row(d), const((1, d))],
        out_specs=row(d),
        out_shape=jax.ShapeDtypeStruct((m, d), F32),
        compiler_params=_params("parallel"),
        name="out_proj",
    )(a2, b2, c2, wa, wb, wc, x2, g.reshape(1, d))


def _ffn_kernel(x_ref, gpre_ref, wi_ref, wo_ref, gpost_ref, o_ref, *, tf):
    x = x_ref[...]
    h = _rms(x, gpre_ref[...]).astype(BF16)
    f = None
    for c in range(D_FF // tf):
        gate = _dot(h, wi_ref[:, c * tf:(c + 1) * tf])
        up = _dot(h, wi_ref[:, D_FF + c * tf:D_FF + (c + 1) * tf])
        act = (gate * _sigmoid(gate) * up).astype(BF16)
        part = _dot(act, wo_ref[c * tf:(c + 1) * tf, :])
        f = part if f is None else f + part
    o_ref[...] = x + _rms(f, gpost_ref[...])


def _ffn(x2, g_pre, w_in_bf16, w_out_bf16, g_post, tm=512, tf=256):
    m, d = x2.shape
    resident = lambda shape: pl.BlockSpec(shape, lambda i: (0, 0), pipeline_mode=pl.Buffered(1))
    return pl.pallas_call(
        functools.partial(_ffn_kernel, tf=tf),
        grid=(m // tm,),
        in_specs=[
            pl.BlockSpec((tm, d), lambda i: (i, 0)),
            resident((1, d)),
            resident((d, 2 * D_FF)),
            resident((D_FF, d)),
            resident((1, d)),
        ],
        out_specs=pl.BlockSpec((tm, d), lambda i: (i, 0)),
        out_shape=jax.ShapeDtypeStruct((m, d), F32),
        compiler_params=_params("parallel"),
        name="ffn",
    )(x2, g_pre.reshape(1, d), w_in_bf16, w_out_bf16, g_post.reshape(1, d))


def kernel(x, pre_mix_g, w_in, rwkv_mu, rwkv_w0, rwkv_w2, rwkv_a0, rwkv_a2, rwkv_g2, rwkv_k_k, rwkv_k_a, rwkv_r_k, rwkv_lnx_g, rwkv_lnx_b, gmlp_ln_g, gmlp_ln_b, gmlp_w_s, gmlp_b_s, w_out, post_mix_g, pre_ffn_g, w_ffn_in, w_ffn_out, post_ffn_g):
    b, s, d = x.shape
    m = b * s
    x2 = x.reshape(m, d)
    for l in range(w_in.shape[0]):
        proj = _in_proj(x2, pre_mix_g[l], w_in[l].astype(BF16)).reshape(b, s, D_IN)
        a_out = _rwkv(proj, rwkv_mu[l], rwkv_w0[l], rwkv_w2[l], rwkv_a0[l], rwkv_a2[l], rwkv_g2[l],
                      rwkv_k_k[l], rwkv_k_a[l], rwkv_r_k[l], rwkv_lnx_g[l], rwkv_lnx_b[l])
        b_out = _moba(proj)
        c_out = _gmlp(proj, gmlp_ln_g[l], gmlp_ln_b[l], gmlp_w_s[l], gmlp_b_s[l])
        x2 = _out_proj(a_out.reshape(m, D_RWKV), b_out.reshape(m, D_MOBA), c_out.reshape(m, D_GMLP),
                       w_out[l].astype(BF16), x2, post_mix_g[l])
        x2 = _ffn(x2, pre_ffn_g[l], w_ffn_in[l].astype(BF16), w_ffn_out[l].astype(BF16), post_ffn_g[l])
    return x2.reshape(b, s, d)
```

```python
import functools

import jax
import jax.numpy as jnp
from jax import lax
from jax.experimental import pallas as pl
from jax.experimental.pallas import tpu as pltpu

F32 = jnp.float32
BF16 = jnp.bfloat16

D_MODEL = 1024
HEAD_DIM = 64
RWKV_HEADS = 6
D_RWKV = RWKV_HEADS * HEAD_DIM
DECAY_LORA = 64
AAA_LORA = 64
GATE_LORA = 128
RWKV_LNX_EPS = 64e-5
MOBA_HEADS = 6
D_MOBA = MOBA_HEADS * HEAD_DIM
MOBA_BLOCK = 256
MOBA_TOPK = 3
GMLP_GROUPS = 4
D_GMLP = GMLP_GROUPS * HEAD_DIM
GMLP_CHUNK = 128
D_FF = 2816
NORM_EPS = 1e-6
LOG2_E = 1.4426950408889634
A_COLS = 3 * D_RWKV + DECAY_LORA + AAA_LORA + GATE_LORA
B_COLS = 3 * D_MOBA
C_COLS = 2 * D_GMLP
D_IN = A_COLS + B_COLS + C_COLS

LANE = 128
RWKV_CHUNK = 64
VMEM_LIMIT = 56 * 1024 * 1024

NN = (((1,), (0,)), ((), ()))
NT = (((1,), (1,)), ((), ()))
TN = (((0,), (0,)), ((), ()))
HI = lax.Precision.HIGHEST


def _dot(a, b, dims=NN, precision=None):
    return lax.dot_general(a, b, dims, precision=precision, preferred_element_type=F32)


def _mm(a, b, dims=NN):
    return _dot(a.astype(BF16), b.astype(BF16), dims)


def _split2(x):
    hi = x.astype(BF16)
    lo = (x - hi.astype(F32)).astype(BF16)
    return hi, lo


def _dot_exact_lhs(w_bf16, x):
    hi, lo = _split2(x)
    return _dot(w_bf16, hi) + _dot(w_bf16, lo)


def _sigmoid(x):
    return 1.0 / (1.0 + jnp.exp(-x))


def _rms(x, g):
    return x * lax.rsqrt(jnp.mean(x * x, axis=-1, keepdims=True) + NORM_EPS) * g


def _params(*sem):
    return pltpu.CompilerParams(dimension_semantics=sem, vmem_limit_bytes=VMEM_LIMIT)


def _in_proj_kernel(x_ref, g_ref, w_ref, o_ref, *, tn):
    h = _rms(x_ref[...], g_ref[...]).astype(BF16)
    for c in range(w_ref.shape[1] // tn):
        o_ref[:, c * tn:(c + 1) * tn] = _dot(h, w_ref[:, c * tn:(c + 1) * tn].astype(BF16))


def _in_proj(x2, g, w_all, layer, tm=512, tn=512):
    m, d = x2.shape
    n = w_all.shape[2]
    return pl.pallas_call(
        functools.partial(_in_proj_kernel, tn=tn),
        grid=(m // tm,),
        in_specs=[
            pl.BlockSpec((tm, d), lambda i: (i, 0)),
            pl.BlockSpec((1, d), lambda i: (0, 0), pipeline_mode=pl.Buffered(1)),
            pl.BlockSpec((None, d, n), lambda i: (layer, 0, 0), pipeline_mode=pl.Buffered(1)),
        ],
        out_specs=pl.BlockSpec((tm, n), lambda i: (i, 0)),
        out_shape=jax.ShapeDtypeStruct((m, n), F32),
        compiler_params=_params("parallel"),
        name="in_proj",
    )(x2, g.reshape(1, d), w_all)


def _rwkv_kernel(ya_ref, mu_ref, wlo_ref, w0a0_ref, g2_ref, vecs_ref, tri_ref, hsum_ref, o_ref,
                 state_ref, prev_ref, at_ref, rt_ref, bh_ref, kh_ref, v_ref, cum_ref, y_ref, vr_ref, *, ts):
    nh, hd, c_len = RWKV_HEADS, HEAD_DIM, RWKV_CHUNK

    @pl.when(pl.program_id(1) == 0)
    def _():
        state_ref[...] = jnp.zeros_like(state_ref)
        prev_ref[...] = jnp.zeros_like(prev_ref)

    ya = ya_ref[0]
    row = lax.broadcasted_iota(jnp.int32, ya.shape, 0)
    prev = jnp.where(row == 0, prev_ref[0:1, :], pltpu.roll(ya, 1, axis=0))
    prev_ref[0:1, :] = ya[ts - 1:ts, :]
    xs = ya + mu_ref[...] * (prev - ya)

    r = xs[:, 0:D_RWKV]
    k = xs[:, D_RWKV:2 * D_RWKV]
    v = xs[:, 2 * D_RWKV:3 * D_RWKV]
    wa = xs[:, 3 * D_RWKV:3 * D_RWKV + LANE]
    gd = xs[:, 3 * D_RWKV + LANE:A_COLS]
    lane = lax.broadcasted_iota(jnp.int32, wa.shape, 1)
    wa = jnp.where(lane < DECAY_LORA, jnp.tanh(wa), wa)
    lo = _mm(wa, wlo_ref[...]) + w0a0_ref[...]
    z = -lo[:, 0:D_RWKV]
    softplus = jnp.maximum(z, 0.0) + jnp.log(1.0 + jnp.exp(-jnp.abs(z)))
    logdecay = -jnp.exp(-softplus - 0.5)
    alpha = _sigmoid(lo[:, D_RWKV:2 * D_RWKV])
    gate = _mm(_sigmoid(gd), g2_ref[...])

    k_k = vecs_ref[0:1, :]
    k_a = vecs_ref[1:2, :]
    r_k = vecs_ref[2:3, :]
    lnx_g = vecs_ref[3:4, :]
    lnx_b = vecs_ref[4:5, :]
    hsum = hsum_ref[...]

    kk = k * k_k
    kk = kk / jnp.maximum(jnp.sqrt(_mm(kk * kk, hsum)), 1e-12)
    k2 = k * (1.0 + (alpha - 1.0) * k_a)
    tri = tri_ref[...]
    tr = tri.shape[0]
    cum = jnp.concatenate([_dot_exact_lhs(tri, logdecay[i * tr:(i + 1) * tr]) for i in range(ts // tr)],
                          axis=0)
    e_neg = jnp.exp(-cum)
    at = -kk * jnp.exp(cum - logdecay)
    rt = r * jnp.exp(cum)
    bh = kk * alpha * e_neg
    kh = k2 * e_neg
    zeros_half = jnp.zeros((ts, hd), F32)
    for h in range(nh):
        sl = slice(h * hd, (h + 1) * hd)
        at_ref[h] = at[:, sl]
        rt_ref[h] = rt[:, sl]
        bh_ref[h] = bh[:, sl]
        kh_ref[h] = kh[:, sl]
        v_ref[h] = v[:, sl]
        vr_ref[h] = jnp.concatenate([zeros_half, v[:, sl]], axis=1)
        cum_ref[h] = cum[:, sl]

    ri = lax.broadcasted_iota(jnp.int32, (c_len, 2 * c_len), 0)
    ci = lax.broadcasted_iota(jnp.int32, (c_len, 2 * c_len), 1)
    ci = jnp.where(ci >= c_len, ci - c_len, ci)
    strict = ri > ci
    incl = ri >= ci
    eye = (lax.broadcasted_iota(jnp.int32, (c_len, c_len), 0)
           == lax.broadcasted_iota(jnp.int32, (c_len, c_len), 1))
    zeros_rows = jnp.zeros((c_len, hd), F32)
    n_sq = c_len.bit_length() - 1

    units = [(cc, h) for cc in range(ts // c_len) for h in range(nh)]

    def cut(ref, u):
        cc, h = u
        return ref[h, cc * c_len:(cc + 1) * c_len, :]

    at_c = [cut(at_ref, u) for u in units]
    rt_c = [cut(rt_ref, u) for u in units]
    bh_c = [cut(bh_ref, u) for u in units]
    kh_c = [cut(kh_ref, u) for u in units]
    g_c = [jnp.exp(cum_ref[h, (cc + 1) * c_len - 1:(cc + 1) * c_len, :]) for cc, h in units]
    p_all = [_mm(jnp.concatenate([a, r_], axis=0), jnp.concatenate([b_, k_], axis=0), NT)
             for a, r_, b_, k_ in zip(at_c, rt_c, bh_c, kh_c)]
    top = [jnp.where(strict, pa[0:c_len], 0.0) for pa in p_all]
    bot = [jnp.where(incl, pa[c_len:], 0.0) for pa in p_all]
    x2 = [_mm(t, jnp.concatenate([zeros_rows, cut(v_ref, u)], axis=0)) for t, u in zip(top, units)]
    x = [jnp.concatenate([a, b_], axis=1) for a, b_ in zip(at_c, x2)]
    p = [t[:, 0:c_len] for t in top]
    for it in range(n_sq):
        if it + 1 < n_sq:
            px = [_mm(pi, jnp.concatenate([xi, pi], axis=1)) for pi, xi in zip(p, x)]
            x = [xi + r_[:, 0:2 * hd] for xi, r_ in zip(x, px)]
            p = [r_[:, 2 * hd:] for r_ in px]
        else:
            x = [xi + _mm(pi, xi) for pi, xi in zip(p, x)]
    rhs = [jnp.concatenate([xi, cut(vr_ref, u)], axis=0) for xi, u in zip(x, units)]
    res_y = [_mm(b_, r_) for b_, r_ in zip(bot, rhs)]
    res_m = [_mm(jnp.concatenate([b_ * g, k_ * g], axis=0), r_, TN)
             for b_, k_, g, r_ in zip(bh_c, kh_c, g_c, rhs)]
    r_eff = [r_ + ry[:, 0:hd] for r_, ry in zip(rt_c, res_y)]
    g_mat = [jnp.where(eye, g, 0.0) + rm[:, 0:hd] for g, rm in zip(g_c, res_m)]
    state = [state_ref[h] for h in range(nh)]
    for i, (cc, h) in enumerate(units):
        prod = _mm(jnp.concatenate([r_eff[i], g_mat[i]], axis=0), state[h])
        y_ref[h, cc * c_len:(cc + 1) * c_len, :] = prod[0:c_len] + res_y[i][:, hd:]
        state[h] = prod[c_len:] + res_m[i][:, hd:]
    for h in range(nh):
        state_ref[h] = state[h]

    y = jnp.concatenate([y_ref[h] for h in range(nh)], axis=-1)
    inv_hd = 1.0 / hd
    mean = _mm(y, hsum) * inv_hd
    yc = y - mean
    var = _mm(yc * yc, hsum) * inv_hd
    yn = yc * lax.rsqrt(var + RWKV_LNX_EPS) * lnx_g + lnx_b
    bonus = _mm(r * k2 * r_k, hsum) * v
    o_ref[0] = ((yn + bonus) * gate).astype(o_ref.dtype)


def _rwkv(proj, mu, w0, w2, a0, a2, g2, k_k, k_a, r_k, lnx_g, lnx_b, ts=512):
    b, s, _ = proj.shape
    wlo = jnp.zeros((LANE, 2 * D_RWKV), F32)
    wlo = wlo.at[0:DECAY_LORA, 0:D_RWKV].set(w2).at[DECAY_LORA:LANE, D_RWKV:].set(a2)
    w0a0 = jnp.concatenate([w0, a0]).reshape(1, 2 * D_RWKV)
    vecs = jnp.zeros((8, D_RWKV), F32)
    vecs = vecs.at[0].set(k_k).at[1].set(k_a).at[2].set(r_k.reshape(-1)).at[3].set(lnx_g).at[4].set(lnx_b)
    tri_rows = min(ts, 256)
    ti = jnp.arange(tri_rows)
    tri = ((ti[:, None] >= ti[None, :]) & (ti[:, None] // RWKV_CHUNK == ti[None, :] // RWKV_CHUNK)).astype(BF16)
    hi_ = jnp.arange(D_RWKV) // HEAD_DIM
    hsum = (hi_[:, None] == hi_[None, :]).astype(BF16)
    const = lambda shape: pl.BlockSpec(shape, lambda i, j: (0,) * len(shape))
    head_buf = pltpu.VMEM((RWKV_HEADS, ts, HEAD_DIM), F32)
    return pl.pallas_call(
        functools.partial(_rwkv_kernel, ts=ts),
        grid=(b, s // ts),
        in_specs=[
            pl.BlockSpec((1, ts, A_COLS), lambda i, j: (i, j, 0)),
            const((1, A_COLS)),
            const((LANE, 2 * D_RWKV)),
            const((1, 2 * D_RWKV)),
            const((GATE_LORA, D_RWKV)),
            const((8, D_RWKV)),
            const((tri_rows, tri_rows)),
            const((D_RWKV, D_RWKV)),
        ],
        out_specs=pl.BlockSpec((1, ts, D_RWKV), lambda i, j: (i, j, 0)),
        out_shape=jax.ShapeDtypeStruct((b, s, D_RWKV), BF16),
        scratch_shapes=[
            pltpu.VMEM((RWKV_HEADS, HEAD_DIM, HEAD_DIM), F32),
            pltpu.VMEM((8, A_COLS), F32),
            head_buf, head_buf, head_buf, head_buf, head_buf, head_buf, head_buf,
            pltpu.VMEM((RWKV_HEADS, ts, 2 * HEAD_DIM), F32),
        ],
        compiler_params=_params("parallel", "arbitrary"),
        name="rwkv7",
    )(proj, mu.reshape(1, A_COLS), wlo, w0a0, g2, vecs, tri, hsum)


def _sublane_allreduce(x, op):
    for shift in (4, 2, 1):
        x = op(x, pltpu.roll(x, shift, axis=0))
    return x


def _moba_kernel(q_ref, k_ref, v_ref, o_ref, *, s):
    blk = MOBA_BLOCK
    nb = s // blk
    sub = 8
    q = q_ref[0]
    k = k_ref[0]
    k16 = k.astype(BF16)
    vt16 = v_ref[0].T.astype(BF16)
    lane = lax.broadcasted_iota(jnp.int32, (1, LANE), 1)
    kbar = jnp.concatenate(
        [jnp.mean(k[n * blk:(n + 1) * blk], axis=0, keepdims=True) for n in range(nb)], axis=0)
    key_i = lax.broadcasted_iota(jnp.int32, (blk, blk), 0)
    qry_i = lax.broadcasted_iota(jnp.int32, (blk, blk), 1)
    neg_inf = -jnp.inf
    causal_pen = jnp.where(key_i <= qry_i, 0.0, neg_inf).reshape(blk // sub, sub, blk)
    blk_id = lax.broadcasted_iota(jnp.int32, (nb, blk), 0)
    out_row = lax.broadcasted_iota(jnp.int32, (LANE, blk), 0)
    n_top = min(MOBA_TOPK, nb - 1)

    units = [(qi, hh) for qi in range(nb) for hh in range(2)]
    logits = []
    for qi, hh in units:
        rows = slice(qi * blk, (qi + 1) * blk)
        head_lanes = (lane >= hh * HEAD_DIM) & (lane < (hh + 1) * HEAD_DIM)
        qh = jnp.where(head_lanes, q[rows], 0.0)
        qh16 = (qh * (HEAD_DIM ** -0.5 * LOG2_E)).astype(BF16)
        first = 0 if n_top > 0 else qi
        kq = lambda n: _dot(k16[n * blk:(n + 1) * blk], qh16, NT).reshape(blk // sub, sub, blk)
        lg3 = []
        if qi > first:
            sc = _dot(kbar, qh, NT, HI)
            cnt = jnp.zeros((nb, blk), F32)
            for m in range(qi):
                cm = sc[m:m + 1, :]
                beats = (cm > sc) | ((cm == sc) & (blk_id > m))
                cnt = cnt + jnp.where(beats, 1.0, 0.0)
            pen = jnp.where((cnt < n_top) & (blk_id < qi), 0.0, neg_inf)
            for n in range(qi):
                lg3.append(kq(n) + jnp.broadcast_to(pen[n:n + 1, :], (sub, blk))[None])
        lg3.append(kq(qi) + causal_pen)
        logits.append((first, lg3))
    m_rep = [_sublane_allreduce(functools.reduce(jnp.maximum, [jnp.max(lg, axis=0) for lg in lgs]),
                                jnp.maximum) for _, lgs in logits]
    outs = []
    for (first, lgs), m_u in zip(logits, m_rep):
        l_rep = jnp.zeros((sub, blk), F32)
        acc = jnp.zeros((LANE, blk), F32)
        for i, lg in enumerate(lgs):
            p = jnp.exp2(lg - m_u[None])
            l_rep = l_rep + jnp.sum(p, axis=0)
            kv = first + i
            acc = acc + _dot(vt16[:, kv * blk:(kv + 1) * blk], p.reshape(blk, blk).astype(BF16))
        l_rep = _sublane_allreduce(l_rep, jnp.add)
        outs.append((acc.reshape(LANE // sub, sub, blk) / l_rep[None]).reshape(LANE, blk))
    for qi in range(nb):
        out_t = jnp.where(out_row < HEAD_DIM, outs[2 * qi], outs[2 * qi + 1])
        o_ref[0, qi * blk:(qi + 1) * blk, :] = out_t.T.astype(o_ref.dtype)


def _moba(proj):
    b, s, _ = proj.shape
    n_pair = D_MOBA // LANE
    base = A_COLS // LANE
    spec = lambda off: pl.BlockSpec((1, s, LANE), lambda i, p: (i, 0, base + off + p))
    return pl.pallas_call(
        functools.partial(_moba_kernel, s=s),
        grid=(b, n_pair),
        in_specs=[spec(0), spec(n_pair), spec(2 * n_pair)],
        out_specs=pl.BlockSpec((1, s, LANE), lambda i, p: (i, 0, p)),
        out_shape=jax.ShapeDtypeStruct((b, s, D_MOBA), BF16),
        compiler_params=_params("parallel", "parallel"),
        name="moba",
    )(proj, proj, proj)


def _gelu(x):
    return 0.5 * x * (1.0 + jnp.tanh(0.7978845608028654 * (x + 0.044715 * (x * x * x))))


def _gmlp_kernel(u_ref, v_ref, lng_ref, lnb_ref, ws_ref, bias_ref, o_ref, *, ts):
    ch = GMLP_CHUNK
    u = _gelu(u_ref[0])
    v = _gelu(v_ref[0])
    mean = jnp.mean(v, axis=-1, keepdims=True)
    vc = v - mean
    var = jnp.mean(vc * vc, axis=-1, keepdims=True)
    vn = ((vc * lax.rsqrt(var + NORM_EPS)) * lng_ref[...] + lnb_ref[...]).astype(BF16)
    ri = lax.broadcasted_iota(jnp.int32, (ch, ch), 0)
    ci = lax.broadcasted_iota(jnp.int32, (ch, ch), 1)
    tril = ri >= ci
    w = [jnp.where(tril, ws_ref[g], 0.0).astype(BF16) for g in range(GMLP_GROUPS)]
    lane = lax.broadcasted_iota(jnp.int32, (1, D_GMLP), 1)
    bias = bias_ref[...]
    for c in range(ts // ch):
        rows = slice(c * ch, (c + 1) * ch)
        vchunk = vn[rows]
        mixed = _dot(w[0], vchunk)
        for g in range(1, GMLP_GROUPS):
            mixed = jnp.where(lane >= g * HEAD_DIM, _dot(w[g], vchunk), mixed)
        o_ref[0, rows, :] = (u[rows] * (mixed + bias)).astype(o_ref.dtype)


def _gmlp(proj, ln_g, ln_b, w_s, b_s, ts=512):
    b, s, _ = proj.shape
    base = (A_COLS + B_COLS) // D_GMLP
    bias = jnp.repeat(b_s.T, HEAD_DIM, axis=1)
    const = lambda shape: pl.BlockSpec(shape, lambda i, j: (0,) * len(shape))
    return pl.pallas_call(
        functools.partial(_gmlp_kernel, ts=ts),
        grid=(b, s // ts),
        in_specs=[
            pl.BlockSpec((1, ts, D_GMLP), lambda i, j: (i, j, base)),
            pl.BlockSpec((1, ts, D_GMLP), lambda i, j: (i, j, base + 1)),
            const((1, D_GMLP)),
            const((1, D_GMLP)),
            const((GMLP_GROUPS, GMLP_CHUNK, GMLP_CHUNK)),
            const((GMLP_CHUNK, D_GMLP)),
        ],
        out_specs=pl.BlockSpec((1, ts, D_GMLP), lambda i, j: (i, j, 0)),
        out_shape=jax.ShapeDtypeStruct((b, s, D_GMLP), BF16),
        compiler_params=_params("parallel", "parallel"),
        name="gmlp",
    )(proj, proj, ln_g.reshape(1, D_GMLP), ln_b.reshape(1, D_GMLP), w_s, bias)


def _out_proj_kernel(a_ref, b_ref, c_ref, wa_ref, wb_ref, wc_ref, x_ref, g_ref, o_ref):
    mix = (_dot(a_ref[...], wa_ref[...].astype(BF16)) + _dot(b_ref[...], wb_ref[...].astype(BF16))
           + _dot(c_ref[...], wc_ref[...].astype(BF16)))
    o_ref[...] = x_ref[...] + _rms(mix, g_ref[...])


def _out_proj(a2, b2, c2, w_all, layer, x2, g, tm=512):
    m, d = x2.shape
    row = lambda n: pl.BlockSpec((tm, n), lambda i: (i, 0))
    band = lambda rows, blk_idx: pl.BlockSpec((None, rows, d), lambda i: (layer, blk_idx, 0),
                                              pipeline_mode=pl.Buffered(1))
    return pl.pallas_call(
        _out_proj_kernel,
        grid=(m // tm,),
        in_specs=[row(D_RWKV), row(D_MOBA), row(D_GMLP),
                  band(D_RWKV, 0), band(D_MOBA, D_RWKV // D_MOBA), band(D_GMLP, (D_RWKV + D_MOBA) // D_GMLP),
                  row(d), pl.BlockSpec((1, d), lambda i: (0, 0))],
        out_specs=row(d),
        out_shape=jax.ShapeDtypeStruct((m, d), F32),
        compiler_params=_params("parallel"),
        name="out_proj",
    )(a2, b2, c2, w_all, w_all, w_all, x2, g.reshape(1, d))


def _ffn_kernel(x_ref, gpre_ref, wi_ref, wo_ref, gpost_ref, o_ref, *, tf):
    x = x_ref[...]
    h = _rms(x, gpre_ref[...]).astype(BF16)
    f = None
    for c in range(D_FF // tf):
        gate = _dot(h, wi_ref[:, c * tf:(c + 1) * tf].astype(BF16))
        up = _dot(h, wi_ref[:, D_FF + c * tf:D_FF + (c + 1) * tf].astype(BF16))
        act = (gate * _sigmoid(gate) * up).astype(BF16)
        part = _dot(act, wo_ref[c * tf:(c + 1) * tf, :].astype(BF16))
        f = part if f is None else f + part
    o_ref[...] = x + _rms(f, gpost_ref[...])


def _ffn(x2, g_pre, w_in_all, w_out_all, layer, g_post, tm=512, tf=256):
    m, d = x2.shape
    resident = lambda shape: pl.BlockSpec(shape, lambda i: (0, 0), pipeline_mode=pl.Buffered(1))
    slab = lambda shape: pl.BlockSpec((None,) + shape, lambda i: (layer, 0, 0), pipeline_mode=pl.Buffered(1))
    return pl.pallas_call(
        functools.partial(_ffn_kernel, tf=tf),
        grid=(m // tm,),
        in_specs=[
            pl.BlockSpec((tm, d), lambda i: (i, 0)),
            resident((1, d)),
            slab((d, 2 * D_FF)),
            slab((D_FF, d)),
            resident((1, d)),
        ],
        out_specs=pl.BlockSpec((tm, d), lambda i: (i, 0)),
        out_shape=jax.ShapeDtypeStruct((m, d), F32),
        compiler_params=_params("parallel"),
        name="ffn",
    )(x2, g_pre.reshape(1, d), w_in_all, w_out_all, g_post.reshape(1, d))


def kernel(x, pre_mix_g, w_in, rwkv_mu, rwkv_w0, rwkv_w2, rwkv_a0, rwkv_a2, rwkv_g2, rwkv_k_k, rwkv_k_a, rwkv_r_k, rwkv_lnx_g, rwkv_lnx_b, gmlp_ln_g, gmlp_ln_b, gmlp_w_s, gmlp_b_s, w_out, post_mix_g, pre_ffn_g, w_ffn_in, w_ffn_out, post_ffn_g):
    b, s, d = x.shape
    m = b * s
    x2 = x.reshape(m, d)
    for l in range(w_in.shape[0]):
        proj = _in_proj(x2, pre_mix_g[l], w_in, l).reshape(b, s, D_IN)
        a_out = _rwkv(proj, rwkv_mu[l], rwkv_w0[l], rwkv_w2[l], rwkv_a0[l], rwkv_a2[l], rwkv_g2[l],
                      rwkv_k_k[l], rwkv_k_a[l], rwkv_r_k[l], rwkv_lnx_g[l], rwkv_lnx_b[l])
        b_out = _moba(proj)
        c_out = _gmlp(proj, gmlp_ln_g[l], gmlp_ln_b[l], gmlp_w_s[l], gmlp_b_s[l])
        x2 = _out_proj(a_out.reshape(m, D_RWKV), b_out.reshape(m, D_MOBA), c_out.reshape(m, D_GMLP),
                       w_out, l, x2, post_mix_g[l])
        x2 = _ffn(x2, pre_ffn_g[l], w_ffn_in, w_ffn_out, l, post_ffn_g[l])
    return x2.reshape(b, s, d)
```

```python
import functools

import jax
import jax.numpy as jnp
from jax import lax
from jax.experimental import pallas as pl
from jax.experimental.pallas import tpu as pltpu

F32 = jnp.float32
BF16 = jnp.bfloat16

D_MODEL = 1024
HEAD_DIM = 64
RWKV_HEADS = 6
D_RWKV = RWKV_HEADS * HEAD_DIM
DECAY_LORA = 64
AAA_LORA = 64
GATE_LORA = 128
RWKV_LNX_EPS = 64e-5
MOBA_HEADS = 6
D_MOBA = MOBA_HEADS * HEAD_DIM
MOBA_BLOCK = 256
MOBA_TOPK = 3
GMLP_GROUPS = 4
D_GMLP = GMLP_GROUPS * HEAD_DIM
GMLP_CHUNK = 128
D_FF = 2816
NORM_EPS = 1e-6
LOG2_E = 1.4426950408889634
A_COLS = 3 * D_RWKV + DECAY_LORA + AAA_LORA + GATE_LORA
B_COLS = 3 * D_MOBA
C_COLS = 2 * D_GMLP
D_IN = A_COLS + B_COLS + C_COLS

LANE = 128
RWKV_CHUNK = 64
RWKV_GROUP_CHUNKS = 4
VMEM_LIMIT = 56 * 1024 * 1024

NN = (((1,), (0,)), ((), ()))
NT = (((1,), (1,)), ((), ()))
TN = (((0,), (0,)), ((), ()))
HI = lax.Precision.HIGHEST


def _dot(a, b, dims=NN, precision=None):
    return lax.dot_general(a, b, dims, precision=precision, preferred_element_type=F32)


def _mm(a, b, dims=NN):
    return _dot(a.astype(BF16), b.astype(BF16), dims)


def _split2(x):
    hi = x.astype(BF16)
    lo = (x - hi.astype(F32)).astype(BF16)
    return hi, lo


def _dot_exact_lhs(w_bf16, x):
    hi, lo = _split2(x)
    return _dot(w_bf16, hi) + _dot(w_bf16, lo)


def _sigmoid(x):
    return 1.0 / (1.0 + jnp.exp(-x))


def _rms(x, g):
    return x * lax.rsqrt(jnp.mean(x * x, axis=-1, keepdims=True) + NORM_EPS) * g


def _params(*sem):
    return pltpu.CompilerParams(dimension_semantics=sem, vmem_limit_bytes=VMEM_LIMIT)


def _in_proj_kernel(x_ref, g_ref, w_ref, o_ref, *, tn):
    h = _rms(x_ref[...], g_ref[...]).astype(BF16)
    for c in range(w_ref.shape[1] // tn):
        o_ref[:, c * tn:(c + 1) * tn] = _dot(h, w_ref[:, c * tn:(c + 1) * tn].astype(BF16))


def _in_proj(x2, g, w_all, layer, tm=512, tn=512):
    m, d = x2.shape
    n = w_all.shape[2]
    return pl.pallas_call(
        functools.partial(_in_proj_kernel, tn=tn),
        grid=(m // tm,),
        in_specs=[
            pl.BlockSpec((tm, d), lambda i: (i, 0)),
            pl.BlockSpec((1, d), lambda i: (0, 0), pipeline_mode=pl.Buffered(1)),
            pl.BlockSpec((None, d, n), lambda i: (layer, 0, 0), pipeline_mode=pl.Buffered(1)),
        ],
        out_specs=pl.BlockSpec((tm, n), lambda i: (i, 0)),
        out_shape=jax.ShapeDtypeStruct((m, n), F32),
        compiler_params=_params("parallel"),
        name="in_proj",
    )(x2, g.reshape(1, d), w_all)


def _rwkv_kernel(ya_ref, mu_ref, wlo_ref, w0a0_ref, g2_ref, vecs_ref, tri_ref, hsum_ref, o_ref,
                 state_ref, prev_ref, at_ref, rt_ref, bh_ref, kh_ref, v_ref, cum_ref, y_ref, vr_ref, *, ts):
    nh, hd, c_len = RWKV_HEADS, HEAD_DIM, RWKV_CHUNK

    @pl.when(pl.program_id(1) == 0)
    def _():
        state_ref[...] = jnp.zeros_like(state_ref)
        prev_ref[...] = jnp.zeros_like(prev_ref)

    ya = ya_ref[0]
    row = lax.broadcasted_iota(jnp.int32, ya.shape, 0)
    prev = jnp.where(row == 0, prev_ref[0:1, :], pltpu.roll(ya, 1, axis=0))
    prev_ref[0:1, :] = ya[ts - 1:ts, :]
    xs = ya + mu_ref[...] * (prev - ya)

    r = xs[:, 0:D_RWKV]
    k = xs[:, D_RWKV:2 * D_RWKV]
    v = xs[:, 2 * D_RWKV:3 * D_RWKV]
    wa = xs[:, 3 * D_RWKV:3 * D_RWKV + LANE]
    gd = xs[:, 3 * D_RWKV + LANE:A_COLS]
    lane = lax.broadcasted_iota(jnp.int32, wa.shape, 1)
    wa = jnp.where(lane < DECAY_LORA, jnp.tanh(wa), wa)
    lo = _mm(wa, wlo_ref[...]) + w0a0_ref[...]
    z = -lo[:, 0:D_RWKV]
    softplus = jnp.maximum(z, 0.0) + jnp.log(1.0 + jnp.exp(-jnp.abs(z)))
    logdecay = -jnp.exp(-softplus - 0.5)
    alpha = _sigmoid(lo[:, D_RWKV:2 * D_RWKV])
    gate = _mm(_sigmoid(gd), g2_ref[...])

    k_k = vecs_ref[0:1, :]
    k_a = vecs_ref[1:2, :]
    r_k = vecs_ref[2:3, :]
    lnx_g = vecs_ref[3:4, :]
    lnx_b = vecs_ref[4:5, :]
    hsum = hsum_ref[...]

    kk = k * k_k
    kk = kk / jnp.maximum(jnp.sqrt(_mm(kk * kk, hsum)), 1e-12)
    k2 = k * (1.0 + (alpha - 1.0) * k_a)
    tri = tri_ref[...]
    tr = tri.shape[0]
    cum = jnp.concatenate([_dot_exact_lhs(tri, logdecay[i * tr:(i + 1) * tr]) for i in range(ts // tr)],
                          axis=0)
    e_neg = jnp.exp(-cum)
    at = -kk * jnp.exp(cum - logdecay)
    rt = r * jnp.exp(cum)
    bh = kk * alpha * e_neg
    kh = k2 * e_neg
    zeros_half = jnp.zeros((ts, hd), F32)
    for h in range(nh):
        sl = slice(h * hd, (h + 1) * hd)
        at_ref[h] = at[:, sl]
        rt_ref[h] = rt[:, sl]
        bh_ref[h] = bh[:, sl]
        kh_ref[h] = kh[:, sl]
        v_ref[h] = v[:, sl]
        vr_ref[h] = jnp.concatenate([zeros_half, v[:, sl]], axis=1)
        cum_ref[h] = cum[:, sl]

    ri = lax.broadcasted_iota(jnp.int32, (c_len, 2 * c_len), 0)
    ci = lax.broadcasted_iota(jnp.int32, (c_len, 2 * c_len), 1)
    ci = jnp.where(ci >= c_len, ci - c_len, ci)
    strict = ri > ci
    incl = ri >= ci
    eye = (lax.broadcasted_iota(jnp.int32, (c_len, c_len), 0)
           == lax.broadcasted_iota(jnp.int32, (c_len, c_len), 1))
    zeros_rows = jnp.zeros((c_len, hd), F32)
    n_sq = c_len.bit_length() - 1

    state = [state_ref[h] for h in range(nh)]

    def cut(ref, u):
        cc, h = u
        return ref[h, cc * c_len:(cc + 1) * c_len, :]

    for g0 in range(0, ts // c_len, RWKV_GROUP_CHUNKS):
        units = [(cc, h) for cc in range(g0, g0 + RWKV_GROUP_CHUNKS) for h in range(nh)]
        at_c = [cut(at_ref, u) for u in units]
        rt_c = [cut(rt_ref, u) for u in units]
        bh_c = [cut(bh_ref, u) for u in units]
        kh_c = [cut(kh_ref, u) for u in units]
        g_c = [jnp.exp(cum_ref[h, (cc + 1) * c_len - 1:(cc + 1) * c_len, :]) for cc, h in units]
        p_all = [_mm(jnp.concatenate([a, r_], axis=0), jnp.concatenate([b_, k_], axis=0), NT)
                 for a, r_, b_, k_ in zip(at_c, rt_c, bh_c, kh_c)]
        top = [jnp.where(strict, pa[0:c_len], 0.0) for pa in p_all]
        bot = [jnp.where(incl, pa[c_len:], 0.0) for pa in p_all]
        x2 = [_mm(t, jnp.concatenate([zeros_rows, cut(v_ref, u)], axis=0)) for t, u in zip(top, units)]
        x = [jnp.concatenate([a, b_], axis=1) for a, b_ in zip(at_c, x2)]
        p = [t[:, 0:c_len] for t in top]
        for it in range(n_sq):
            if it + 1 < n_sq:
                px = [_mm(pi, jnp.concatenate([xi, pi], axis=1)) for pi, xi in zip(p, x)]
                x = [xi + r_[:, 0:2 * hd] for xi, r_ in zip(x, px)]
                p = [r_[:, 2 * hd:] for r_ in px]
            else:
                x = [xi + _mm(pi, xi) for pi, xi in zip(p, x)]
        rhs = [jnp.concatenate([xi, cut(vr_ref, u)], axis=0) for xi, u in zip(x, units)]
        res_y = [_mm(b_, r_) for b_, r_ in zip(bot, rhs)]
        res_m = [_mm(jnp.concatenate([b_ * g, k_ * g], axis=0), r_, TN)
                 for b_, k_, g, r_ in zip(bh_c, kh_c, g_c, rhs)]
        r_eff = [r_ + ry[:, 0:hd] for r_, ry in zip(rt_c, res_y)]
        g_mat = [jnp.where(eye, g, 0.0) + rm[:, 0:hd] for g, rm in zip(g_c, res_m)]
        for i, (cc, h) in enumerate(units):
            prod = _mm(jnp.concatenate([r_eff[i], g_mat[i]], axis=0), state[h])
            y_ref[h, cc * c_len:(cc + 1) * c_len, :] = prod[0:c_len] + res_y[i][:, hd:]
            state[h] = prod[c_len:] + res_m[i][:, hd:]
    for h in range(nh):
        state_ref[h] = state[h]

    y = jnp.concatenate([y_ref[h] for h in range(nh)], axis=-1)
    inv_hd = 1.0 / hd
    mean = _mm(y, hsum) * inv_hd
    yc = y - mean
    var = _mm(yc * yc, hsum) * inv_hd
    yn = yc * lax.rsqrt(var + RWKV_LNX_EPS) * lnx_g + lnx_b
    bonus = _mm(r * k2 * r_k, hsum) * v
    o_ref[0] = ((yn + bonus) * gate).astype(o_ref.dtype)


def _rwkv(proj, mu, w0, w2, a0, a2, g2, k_k, k_a, r_k, lnx_g, lnx_b, ts=512):
    b, s, _ = proj.shape
    wlo = jnp.zeros((LANE, 2 * D_RWKV), F32)
    wlo = wlo.at[0:DECAY_LORA, 0:D_RWKV].set(w2).at[DECAY_LORA:LANE, D_RWKV:].set(a2)
    w0a0 = jnp.concatenate([w0, a0]).reshape(1, 2 * D_RWKV)
    vecs = jnp.zeros((8, D_RWKV), F32)
    vecs = vecs.at[0].set(k_k).at[1].set(k_a).at[2].set(r_k.reshape(-1)).at[3].set(lnx_g).at[4].set(lnx_b)
    tri_rows = min(ts, 256)
    ti = jnp.arange(tri_rows)
    tri = ((ti[:, None] >= ti[None, :]) & (ti[:, None] // RWKV_CHUNK == ti[None, :] // RWKV_CHUNK)).astype(BF16)
    hi_ = jnp.arange(D_RWKV) // HEAD_DIM
    hsum = (hi_[:, None] == hi_[None, :]).astype(BF16)
    const = lambda shape: pl.BlockSpec(shape, lambda i, j: (0,) * len(shape))
    head_buf = pltpu.VMEM((RWKV_HEADS, ts, HEAD_DIM), F32)
    return pl.pallas_call(
        functools.partial(_rwkv_kernel, ts=ts),
        grid=(b, s // ts),
        in_specs=[
            pl.BlockSpec((1, ts, A_COLS), lambda i, j: (i, j, 0)),
            const((1, A_COLS)),
            const((LANE, 2 * D_RWKV)),
            const((1, 2 * D_RWKV)),
            const((GATE_LORA, D_RWKV)),
            const((8, D_RWKV)),
            const((tri_rows, tri_rows)),
            const((D_RWKV, D_RWKV)),
        ],
        out_specs=pl.BlockSpec((1, ts, D_RWKV), lambda i, j: (i, j, 0)),
        out_shape=jax.ShapeDtypeStruct((b, s, D_RWKV), BF16),
        scratch_shapes=[
            pltpu.VMEM((RWKV_HEADS, HEAD_DIM, HEAD_DIM), F32),
            pltpu.VMEM((8, A_COLS), F32),
            head_buf, head_buf, head_buf, head_buf, head_buf, head_buf, head_buf,
            pltpu.VMEM((RWKV_HEADS, ts, 2 * HEAD_DIM), F32),
        ],
        compiler_params=_params("parallel", "arbitrary"),
        name="rwkv7",
    )(proj, mu.reshape(1, A_COLS), wlo, w0a0, g2, vecs, tri, hsum)


def _sublane_allreduce(x, op):
    for shift in (4, 2, 1):
        x = op(x, pltpu.roll(x, shift, axis=0))
    return x


def _moba_kernel(q_ref, k_ref, v_ref, o_ref, *, s):
    blk = MOBA_BLOCK
    nb = s // blk
    sub = 8
    q = q_ref[0]
    k = k_ref[0]
    k16 = k.astype(BF16)
    vt16 = v_ref[0].T.astype(BF16)
    lane = lax.broadcasted_iota(jnp.int32, (1, LANE), 1)
    kbar = jnp.concatenate(
        [jnp.mean(k[n * blk:(n + 1) * blk], axis=0, keepdims=True) for n in range(nb)], axis=0)
    key_i = lax.broadcasted_iota(jnp.int32, (blk, blk), 0)
    qry_i = lax.broadcasted_iota(jnp.int32, (blk, blk), 1)
    neg_inf = -jnp.inf
    causal_pen = jnp.where(key_i <= qry_i, 0.0, neg_inf).reshape(blk // sub, sub, blk)
    blk_id = lax.broadcasted_iota(jnp.int32, (nb, blk), 0)
    out_row = lax.broadcasted_iota(jnp.int32, (LANE, blk), 0)
    n_top = min(MOBA_TOPK, nb - 1)

    q_groups = [[qi for qi in range(nb) if qi % 4 in (0, 3)], [qi for qi in range(nb) if qi % 4 in (1, 2)]]
    for group in q_groups:
        units = [(qi, hh) for qi in group for hh in range(2)]
        logits = []
        for qi, hh in units:
            rows = slice(qi * blk, (qi + 1) * blk)
            head_lanes = (lane >= hh * HEAD_DIM) & (lane < (hh + 1) * HEAD_DIM)
            qh = jnp.where(head_lanes, q[rows], 0.0)
            qh16 = (qh * (HEAD_DIM ** -0.5 * LOG2_E)).astype(BF16)
            first = 0 if n_top > 0 else qi
            kq = lambda n: _dot(k16[n * blk:(n + 1) * blk], qh16, NT).reshape(blk // sub, sub, blk)
            lg3 = []
            if qi > first:
                sc = _dot(kbar, qh, NT, HI)
                cnt = jnp.zeros((nb, blk), F32)
                for m in range(qi):
                    cm = sc[m:m + 1, :]
                    beats = (cm > sc) | ((cm == sc) & (blk_id > m))
                    cnt = cnt + jnp.where(beats, 1.0, 0.0)
                pen = jnp.where((cnt < n_top) & (blk_id < qi), 0.0, neg_inf)
                for n in range(qi):
                    lg3.append(kq(n) + jnp.broadcast_to(pen[n:n + 1, :], (sub, blk))[None])
            lg3.append(kq(qi) + causal_pen)
            logits.append((first, lg3))
        m_rep = [_sublane_allreduce(functools.reduce(jnp.maximum, [jnp.max(lg, axis=0) for lg in lgs]),
                                    jnp.maximum) for _, lgs in logits]
        outs = []
        for (first, lgs), m_u in zip(logits, m_rep):
            l_rep = jnp.zeros((sub, blk), F32)
            acc = jnp.zeros((LANE, blk), F32)
            for i, lg in enumerate(lgs):
                p = jnp.exp2(lg - m_u[None])
                l_rep = l_rep + jnp.sum(p, axis=0)
                kv = first + i
                acc = acc + _dot(vt16[:, kv * blk:(kv + 1) * blk], p.reshape(blk, blk).astype(BF16))
            l_rep = _sublane_allreduce(l_rep, jnp.add)
            outs.append((acc.reshape(LANE // sub, sub, blk) / l_rep[None]).reshape(LANE, blk))
        for j, qi in enumerate(group):
            out_t = jnp.where(out_row < HEAD_DIM, outs[2 * j], outs[2 * j + 1])
            o_ref[0, qi * blk:(qi + 1) * blk, :] = out_t.T.astype(o_ref.dtype)


def _moba(proj):
    b, s, _ = proj.shape
    n_pair = D_MOBA // LANE
    base = A_COLS // LANE
    spec = lambda off: pl.BlockSpec((1, s, LANE), lambda i, p: (i, 0, base + off + p))
    return pl.pallas_call(
        functools.partial(_moba_kernel, s=s),
        grid=(b, n_pair),
        in_specs=[spec(0), spec(n_pair), spec(2 * n_pair)],
        out_specs=pl.BlockSpec((1, s, LANE), lambda i, p: (i, 0, p)),
        out_shape=jax.ShapeDtypeStruct((b, s, D_MOBA), BF16),
        compiler_params=_params("parallel", "parallel"),
        name="moba",
    )(proj, proj, proj)


def _gelu(x):
    return 0.5 * x * (1.0 + jnp.tanh(0.7978845608028654 * (x + 0.044715 * (x * x * x))))


def _gmlp_kernel(u_ref, v_ref, lng_ref, lnb_ref, ws_ref, bias_ref, o_ref, *, ts):
    ch = GMLP_CHUNK
    u = _gelu(u_ref[0])
    v = _gelu(v_ref[0])
    mean = jnp.mean(v, axis=-1, keepdims=True)
    vc = v - mean
    var = jnp.mean(vc * vc, axis=-1, keepdims=True)
    vn = ((vc * lax.rsqrt(var + NORM_EPS)) * lng_ref[...] + lnb_ref[...]).astype(BF16)
    ri = lax.broadcasted_iota(jnp.int32, (ch, ch), 0)
    ci = lax.broadcasted_iota(jnp.int32, (ch, ch), 1)
    tril = ri >= ci
    w = [jnp.where(tril, ws_ref[g], 0.0).astype(BF16) for g in range(GMLP_GROUPS)]
    lane = lax.broadcasted_iota(jnp.int32, (1, D_GMLP), 1)
    bias = bias_ref[...]
    for c in range(ts // ch):
        rows = slice(c * ch, (c + 1) * ch)
        vchunk = vn[rows]
        mixed = _dot(w[0], vchunk)
        for g in range(1, GMLP_GROUPS):
            mixed = jnp.where(lane >= g * HEAD_DIM, _dot(w[g], vchunk), mixed)
        o_ref[0, rows, :] = (u[rows] * (mixed + bias)).astype(o_ref.dtype)


def _gmlp(proj, ln_g, ln_b, w_s, b_s, ts=512):
    b, s, _ = proj.shape
    base = (A_COLS + B_COLS) // D_GMLP
    bias = jnp.repeat(b_s.T, HEAD_DIM, axis=1)
    const = lambda shape: pl.BlockSpec(shape, lambda i, j: (0,) * len(shape))
    return pl.pallas_call(
        functools.partial(_gmlp_kernel, ts=ts),
        grid=(b, s // ts),
        in_specs=[
            pl.BlockSpec((1, ts, D_GMLP), lambda i, j: (i, j, base)),
            pl.BlockSpec((1, ts, D_GMLP), lambda i, j: (i, j, base + 1)),
            const((1, D_GMLP)),
            const((1, D_GMLP)),
            const((GMLP_GROUPS, GMLP_CHUNK, GMLP_CHUNK)),
            const((GMLP_CHUNK, D_GMLP)),
        ],
        out_specs=pl.BlockSpec((1, ts, D_GMLP), lambda i, j: (i, j, 0)),
        out_shape=jax.ShapeDtypeStruct((b, s, D_GMLP), BF16),
        compiler_params=_params("parallel", "parallel"),
        name="gmlp",
    )(proj, proj, ln_g.reshape(1, D_GMLP), ln_b.reshape(1, D_GMLP), w_s, bias)


def _out_proj_kernel(a_ref, b_ref, c_ref, wa_ref, wb_ref, wc_ref, x_ref, g_ref, o_ref):
    mix = (_dot(a_ref[...], wa_ref[...].astype(BF16)) + _dot(b_ref[...], wb_ref[...].astype(BF16))
           + _dot(c_ref[...], wc_ref[...].astype(BF16)))
    o_ref[...] = x_ref[...] + _rms(mix, g_ref[...])


def _out_proj(a2, b2, c2, w_all, layer, x2, g, tm=512):
    m, d = x2.shape
    row = lambda n: pl.BlockSpec((tm, n), lambda i: (i, 0))
    band = lambda rows, blk_idx: pl.BlockSpec((None, rows, d), lambda i: (layer, blk_idx, 0),
                                              pipeline_mode=pl.Buffered(1))
    return pl.pallas_call(
        _out_proj_kernel,
        grid=(m // tm,),
        in_specs=[row(D_RWKV), row(D_MOBA), row(D_GMLP),
                  band(D_RWKV, 0), band(D_MOBA, D_RWKV // D_MOBA), band(D_GMLP, (D_RWKV + D_MOBA) // D_GMLP),
                  row(d), pl.BlockSpec((1, d), lambda i: (0, 0))],
        out_specs=row(d),
        out_shape=jax.ShapeDtypeStruct((m, d), F32),
        compiler_params=_params("parallel"),
        name="out_proj",
    )(a2, b2, c2, w_all, w_all, w_all, x2, g.reshape(1, d))


def _ffn_kernel(x_ref, gpre_ref, wi_ref, wo_ref, gpost_ref, o_ref, *, tf):
    x = x_ref[...]
    h = _rms(x, gpre_ref[...]).astype(BF16)
    f = None
    for c in range(D_FF // tf):
        gate = _dot(h, wi_ref[:, c * tf:(c + 1) * tf].astype(BF16))
        up = _dot(h, wi_ref[:, D_FF + c * tf:D_FF + (c + 1) * tf].astype(BF16))
        act = (gate * _sigmoid(gate) * up).astype(BF16)
        part = _dot(act, wo_ref[c * tf:(c + 1) * tf, :].astype(BF16))
        f = part if f is None else f + part
    o_ref[...] = x + _rms(f, gpost_ref[...])


def _ffn(x2, g_pre, w_in_all, w_out_all, layer, g_post, tm=512, tf=256):
    m, d = x2.shape
    resident = lambda shape: pl.BlockSpec(shape, lambda i: (0, 0), pipeline_mode=pl.Buffered(1))
    slab = lambda shape: pl.BlockSpec((None,) + shape, lambda i: (layer, 0, 0), pipeline_mode=pl.Buffered(1))
    return pl.pallas_call(
        functools.partial(_ffn_kernel, tf=tf),
        grid=(m // tm,),
        in_specs=[
            pl.BlockSpec((tm, d), lambda i: (i, 0)),
            resident((1, d)),
            slab((d, 2 * D_FF)),
            slab((D_FF, d)),
            resident((1, d)),
        ],
        out_specs=pl.BlockSpec((tm, d), lambda i: (i, 0)),
        out_shape=jax.ShapeDtypeStruct((m, d), F32),
        compiler_params=_params("parallel"),
        name="ffn",
    )(x2, g_pre.reshape(1, d), w_in_all, w_out_all, g_post.reshape(1, d))


def kernel(x, pre_mix_g, w_in, rwkv_mu, rwkv_w0, rwkv_w2, rwkv_a0, rwkv_a2, rwkv_g2, rwkv_k_k, rwkv_k_a, rwkv_r_k, rwkv_lnx_g, rwkv_lnx_b, gmlp_ln_g, gmlp_ln_b, gmlp_w_s, gmlp_b_s, w_out, post_mix_g, pre_ffn_g, w_ffn_in, w_ffn_out, post_ffn_g):
    b, s, d = x.shape
    m = b * s
    x2 = x.reshape(m, d)
    for l in range(w_in.shape[0]):
        proj = _in_proj(x2, pre_mix_g[l], w_in, l).reshape(b, s, D_IN)
        a_out = _rwkv(proj, rwkv_mu[l], rwkv_w0[l], rwkv_w2[l], rwkv_a0[l], rwkv_a2[l], rwkv_g2[l],
                      rwkv_k_k[l], rwkv_k_a[l], rwkv_r_k[l], rwkv_lnx_g[l], rwkv_lnx_b[l])
        b_out = _moba(proj)
        c_out = _gmlp(proj, gmlp_ln_g[l], gmlp_ln_b[l], gmlp_w_s[l], gmlp_b_s[l])
        x2 = _out_proj(a_out.reshape(m, D_RWKV), b_out.reshape(m, D_MOBA), c_out.reshape(m, D_GMLP),
                       w_out, l, x2, post_mix_g[l])
        x2 = _ffn(x2, pre_ffn_g[l], w_ffn_in, w_ffn_out, l, post_ffn_g[l])
    return x2.reshape(b, s, d)
```

```python
import functools

import jax
import jax.numpy as jnp
from jax import lax
from jax.experimental import pallas as pl
from jax.experimental.pallas import tpu as pltpu

F32 = jnp.float32
BF16 = jnp.bfloat16

D_MODEL = 1024
HEAD_DIM = 64
RWKV_HEADS = 6
D_RWKV = RWKV_HEADS * HEAD_DIM
DECAY_LORA = 64
AAA_LORA = 64
GATE_LORA = 128
RWKV_LNX_EPS = 64e-5
MOBA_HEADS = 6
D_MOBA = MOBA_HEADS * HEAD_DIM
MOBA_BLOCK = 256
MOBA_TOPK = 3
GMLP_GROUPS = 4
D_GMLP = GMLP_GROUPS * HEAD_DIM
GMLP_CHUNK = 128
D_FF = 2816
NORM_EPS = 1e-6
LOG2_E = 1.4426950408889634
A_COLS = 3 * D_RWKV + DECAY_LORA + AAA_LORA + GATE_LORA
B_COLS = 3 * D_MOBA
C_COLS = 2 * D_GMLP
D_IN = A_COLS + B_COLS + C_COLS

LANE = 128
RWKV_CHUNK = 64
RWKV_GROUP_CHUNKS = 4
VMEM_LIMIT = 56 * 1024 * 1024

NN = (((1,), (0,)), ((), ()))
NT = (((1,), (1,)), ((), ()))
TN = (((0,), (0,)), ((), ()))
HI = lax.Precision.HIGHEST


def _dot(a, b, dims=NN, precision=None):
    return lax.dot_general(a, b, dims, precision=precision, preferred_element_type=F32)


def _mm(a, b, dims=NN):
    return _dot(a.astype(BF16), b.astype(BF16), dims)


def _split2(x):
    hi = x.astype(BF16)
    lo = (x - hi.astype(F32)).astype(BF16)
    return hi, lo


def _dot_exact_lhs(w_bf16, x):
    hi, lo = _split2(x)
    return _dot(w_bf16, hi) + _dot(w_bf16, lo)


def _sigmoid(x):
    return 1.0 / (1.0 + jnp.exp(-x))


def _rms(x, g):
    return x * lax.rsqrt(jnp.mean(x * x, axis=-1, keepdims=True) + NORM_EPS) * g


def _params(*sem):
    return pltpu.CompilerParams(dimension_semantics=sem, vmem_limit_bytes=VMEM_LIMIT)


def _in_proj_kernel(x_ref, g_ref, w_ref, o_ref, *, tn):
    h = _rms(x_ref[...], g_ref[...]).astype(BF16)
    for c in range(w_ref.shape[1] // tn):
        o_ref[:, c * tn:(c + 1) * tn] = _dot(h, w_ref[:, c * tn:(c + 1) * tn].astype(BF16))


def _in_proj(x2, g, w_all, layer, tm=512, tn=512):
    m, d = x2.shape
    n = w_all.shape[2]
    return pl.pallas_call(
        functools.partial(_in_proj_kernel, tn=tn),
        grid=(m // tm,),
        in_specs=[
            pl.BlockSpec((tm, d), lambda i: (i, 0)),
            pl.BlockSpec((1, d), lambda i: (0, 0), pipeline_mode=pl.Buffered(1)),
            pl.BlockSpec((None, d, n), lambda i: (layer, 0, 0), pipeline_mode=pl.Buffered(1)),
        ],
        out_specs=pl.BlockSpec((tm, n), lambda i: (i, 0)),
        out_shape=jax.ShapeDtypeStruct((m, n), F32),
        compiler_params=_params("parallel"),
        name="in_proj",
    )(x2, g.reshape(1, d), w_all)


def _rwkv_kernel(ya_ref, mu_ref, wlo_ref, w0a0_ref, g2_ref, vecs_ref, tri_ref, hsum_ref, o_ref,
                 state_ref, prev_ref, at_ref, rt_ref, bh_ref, kh_ref, v_ref, cum_ref, y_ref, vr_ref, *, ts):
    nh, hd, c_len = RWKV_HEADS, HEAD_DIM, RWKV_CHUNK

    @pl.when(pl.program_id(1) == 0)
    def _():
        state_ref[...] = jnp.zeros_like(state_ref)
        prev_ref[...] = jnp.zeros_like(prev_ref)

    ya = ya_ref[0]
    row = lax.broadcasted_iota(jnp.int32, ya.shape, 0)
    prev = jnp.where(row == 0, prev_ref[0:1, :], pltpu.roll(ya, 1, axis=0))
    prev_ref[0:1, :] = ya[ts - 1:ts, :]
    xs = ya + mu_ref[...] * (prev - ya)

    r = xs[:, 0:D_RWKV]
    k = xs[:, D_RWKV:2 * D_RWKV]
    v = xs[:, 2 * D_RWKV:3 * D_RWKV]
    wa = xs[:, 3 * D_RWKV:3 * D_RWKV + LANE]
    gd = xs[:, 3 * D_RWKV + LANE:A_COLS]
    lane = lax.broadcasted_iota(jnp.int32, wa.shape, 1)
    wa = jnp.where(lane < DECAY_LORA, jnp.tanh(wa), wa)
    lo = _mm(wa, wlo_ref[...]) + w0a0_ref[...]
    z = -lo[:, 0:D_RWKV]
    softplus = jnp.maximum(z, 0.0) + jnp.log(1.0 + jnp.exp(-jnp.abs(z)))
    logdecay = -jnp.exp(-softplus - 0.5)
    alpha = _sigmoid(lo[:, D_RWKV:2 * D_RWKV])
    gate = _mm(_sigmoid(gd), g2_ref[...])

    k_k = vecs_ref[0:1, :]
    k_a = vecs_ref[1:2, :]
    r_k = vecs_ref[2:3, :]
    lnx_g = vecs_ref[3:4, :]
    lnx_b = vecs_ref[4:5, :]
    hsum = hsum_ref[...]

    kk = k * k_k
    kk = kk / jnp.maximum(jnp.sqrt(_mm(kk * kk, hsum)), 1e-12)
    k2 = k * (1.0 + (alpha - 1.0) * k_a)
    tri = tri_ref[...]
    tr = tri.shape[0]
    cum = jnp.concatenate([_dot_exact_lhs(tri, logdecay[i * tr:(i + 1) * tr]) for i in range(ts // tr)],
                          axis=0)
    e_neg = jnp.exp(-cum)
    at = -kk * jnp.exp(cum - logdecay)
    rt = r * jnp.exp(cum)
    bh = kk * alpha * e_neg
    kh = k2 * e_neg
    zeros_half = jnp.zeros((ts, hd), F32)
    for h in range(nh):
        sl = slice(h * hd, (h + 1) * hd)
        at_ref[h] = at[:, sl]
        rt_ref[h] = rt[:, sl]
        bh_ref[h] = bh[:, sl]
        kh_ref[h] = kh[:, sl]
        v_ref[h] = v[:, sl]
        vr_ref[h] = jnp.concatenate([zeros_half, v[:, sl]], axis=1)
        cum_ref[h] = cum[:, sl]

    ri = lax.broadcasted_iota(jnp.int32, (c_len, 2 * c_len), 0)
    ci = lax.broadcasted_iota(jnp.int32, (c_len, 2 * c_len), 1)
    ci = jnp.where(ci >= c_len, ci - c_len, ci)
    strict = ri > ci
    incl = ri >= ci
    eye = (lax.broadcasted_iota(jnp.int32, (c_len, c_len), 0)
           == lax.broadcasted_iota(jnp.int32, (c_len, c_len), 1))
    zeros_rows = jnp.zeros((c_len, hd), F32)
    n_sq = c_len.bit_length() - 1

    state = [state_ref[h] for h in range(nh)]

    def cut(ref, u):
        cc, h = u
        return ref[h, cc * c_len:(cc + 1) * c_len, :]

    for g0 in range(0, ts // c_len, RWKV_GROUP_CHUNKS):
        units = [(cc, h) for cc in range(g0, g0 + RWKV_GROUP_CHUNKS) for h in range(nh)]
        at_c = [cut(at_ref, u) for u in units]
        rt_c = [cut(rt_ref, u) for u in units]
        bh_c = [cut(bh_ref, u) for u in units]
        kh_c = [cut(kh_ref, u) for u in units]
        g_c = [jnp.exp(cum_ref[h, (cc + 1) * c_len - 1:(cc + 1) * c_len, :]) for cc, h in units]
        p_all = [_mm(jnp.concatenate([a, r_], axis=0), jnp.concatenate([b_, k_], axis=0), NT)
                 for a, r_, b_, k_ in zip(at_c, rt_c, bh_c, kh_c)]
        top = [jnp.where(strict, pa[0:c_len], 0.0) for pa in p_all]
        bot = [jnp.where(incl, pa[c_len:], 0.0) for pa in p_all]
        x2 = [_mm(t, jnp.concatenate([zeros_rows, cut(v_ref, u)], axis=0)) for t, u in zip(top, units)]
        x = [jnp.concatenate([a, b_], axis=1) for a, b_ in zip(at_c, x2)]
        p = [t[:, 0:c_len] for t in top]
        for it in range(n_sq):
            if it + 1 < n_sq:
                px = [_mm(pi, jnp.concatenate([xi, pi], axis=1)) for pi, xi in zip(p, x)]
                x = [xi + r_[:, 0:2 * hd] for xi, r_ in zip(x, px)]
                p = [r_[:, 2 * hd:] for r_ in px]
            else:
                x = [xi + _mm(pi, xi) for pi, xi in zip(p, x)]
        rhs = [jnp.concatenate([xi, cut(vr_ref, u)], axis=0) for xi, u in zip(x, units)]
        res_y = [_mm(b_, r_) for b_, r_ in zip(bot, rhs)]
        res_m = [_mm(jnp.concatenate([b_ * g, k_ * g], axis=0), r_, TN)
                 for b_, k_, g, r_ in zip(bh_c, kh_c, g_c, rhs)]
        r_eff = [r_ + ry[:, 0:hd] for r_, ry in zip(rt_c, res_y)]
        g_mat = [jnp.where(eye, g, 0.0) + rm[:, 0:hd] for g, rm in zip(g_c, res_m)]
        for i, (cc, h) in enumerate(units):
            prod = _mm(jnp.concatenate([r_eff[i], g_mat[i]], axis=0), state[h])
            y_ref[h, cc * c_len:(cc + 1) * c_len, :] = prod[0:c_len] + res_y[i][:, hd:]
            state[h] = prod[c_len:] + res_m[i][:, hd:]
    for h in range(nh):
        state_ref[h] = state[h]

    y = jnp.concatenate([y_ref[h] for h in range(nh)], axis=-1)
    inv_hd = 1.0 / hd
    mean = _mm(y, hsum) * inv_hd
    yc = y - mean
    var = _mm(yc * yc, hsum) * inv_hd
    yn = yc * lax.rsqrt(var + RWKV_LNX_EPS) * lnx_g + lnx_b
    bonus = _mm(r * k2 * r_k, hsum) * v
    o_ref[0] = ((yn + bonus) * gate).astype(o_ref.dtype)


def _rwkv(proj, mu, w0, w2, a0, a2, g2, k_k, k_a, r_k, lnx_g, lnx_b, ts=512):
    b, s, _ = proj.shape
    wlo = jnp.zeros((LANE, 2 * D_RWKV), F32)
    wlo = wlo.at[0:DECAY_LORA, 0:D_RWKV].set(w2).at[DECAY_LORA:LANE, D_RWKV:].set(a2)
    w0a0 = jnp.concatenate([w0, a0]).reshape(1, 2 * D_RWKV)
    vecs = jnp.zeros((8, D_RWKV), F32)
    vecs = vecs.at[0].set(k_k).at[1].set(k_a).at[2].set(r_k.reshape(-1)).at[3].set(lnx_g).at[4].set(lnx_b)
    tri_rows = min(ts, 256)
    ti = jnp.arange(tri_rows)
    tri = ((ti[:, None] >= ti[None, :]) & (ti[:, None] // RWKV_CHUNK == ti[None, :] // RWKV_CHUNK)).astype(BF16)
    hi_ = jnp.arange(D_RWKV) // HEAD_DIM
    hsum = (hi_[:, None] == hi_[None, :]).astype(BF16)
    const = lambda shape: pl.BlockSpec(shape, lambda i, j: (0,) * len(shape))
    head_buf = pltpu.VMEM((RWKV_HEADS, ts, HEAD_DIM), F32)
    return pl.pallas_call(
        functools.partial(_rwkv_kernel, ts=ts),
        grid=(b, s // ts),
        in_specs=[
            pl.BlockSpec((1, ts, A_COLS), lambda i, j: (i, j, 0)),
            const((1, A_COLS)),
            const((LANE, 2 * D_RWKV)),
            const((1, 2 * D_RWKV)),
            const((GATE_LORA, D_RWKV)),
            const((8, D_RWKV)),
            const((tri_rows, tri_rows)),
            const((D_RWKV, D_RWKV)),
        ],
        out_specs=pl.BlockSpec((1, ts, D_RWKV), lambda i, j: (i, j, 0)),
        out_shape=jax.ShapeDtypeStruct((b, s, D_RWKV), BF16),
        scratch_shapes=[
            pltpu.VMEM((RWKV_HEADS, HEAD_DIM, HEAD_DIM), F32),
            pltpu.VMEM((8, A_COLS), F32),
            head_buf, head_buf, head_buf, head_buf, head_buf, head_buf, head_buf,
            pltpu.VMEM((RWKV_HEADS, ts, 2 * HEAD_DIM), F32),
        ],
        compiler_params=_params("parallel", "arbitrary"),
        name="rwkv7",
    )(proj, mu.reshape(1, A_COLS), wlo, w0a0, g2, vecs, tri, hsum)


def _sublane_allreduce(x, op):
    for shift in (4, 2, 1):
        x = op(x, pltpu.roll(x, shift, axis=0))
    return x


def _moba_kernel(q_ref, k_ref, v_ref, o_ref, *, s):
    blk = MOBA_BLOCK
    nb = s // blk
    sub = 8
    q = q_ref[0]
    k = k_ref[0]
    k16 = k.astype(BF16)
    vt16 = jnp.concatenate([v_ref[0].T, jnp.ones((sub, s), F32)], axis=0).astype(BF16)
    lane = lax.broadcasted_iota(jnp.int32, (1, LANE), 1)
    kbar = jnp.concatenate(
        [jnp.mean(k[n * blk:(n + 1) * blk], axis=0, keepdims=True) for n in range(nb)], axis=0)
    key_i = lax.broadcasted_iota(jnp.int32, (blk, blk), 0)
    qry_i = lax.broadcasted_iota(jnp.int32, (blk, blk), 1)
    neg_inf = -jnp.inf
    causal_pen = jnp.where(key_i <= qry_i, 0.0, neg_inf).reshape(blk // sub, sub, blk)
    blk_id = lax.broadcasted_iota(jnp.int32, (nb, blk), 0)
    out_row = lax.broadcasted_iota(jnp.int32, (LANE, blk), 0)
    n_top = min(MOBA_TOPK, nb - 1)

    q_groups = [[qi for qi in range(nb) if qi % 4 in (0, 3)], [qi for qi in range(nb) if qi % 4 in (1, 2)]]
    for group in q_groups:
        units = [(qi, hh) for qi in group for hh in range(2)]
        logits = []
        for qi, hh in units:
            rows = slice(qi * blk, (qi + 1) * blk)
            head_lanes = (lane >= hh * HEAD_DIM) & (lane < (hh + 1) * HEAD_DIM)
            qh = jnp.where(head_lanes, q[rows], 0.0)
            qh16 = (qh * (HEAD_DIM ** -0.5 * LOG2_E)).astype(BF16)
            first = 0 if n_top > 0 else qi
            kq = lambda n: _dot(k16[n * blk:(n + 1) * blk], qh16, NT).reshape(blk // sub, sub, blk)
            lg3 = []
            if qi > first:
                sc = _dot(kbar, qh, NT, HI)
                cnt = jnp.zeros((nb, blk), F32)
                for m in range(qi):
                    cm = sc[m:m + 1, :]
                    beats = (cm > sc) | ((cm == sc) & (blk_id > m))
                    cnt = cnt + jnp.where(beats, 1.0, 0.0)
                pen = jnp.where((cnt < n_top) & (blk_id < qi), 0.0, neg_inf)
                for n in range(qi):
                    lg3.append(kq(n) + jnp.broadcast_to(pen[n:n + 1, :], (sub, blk))[None])
            lg3.append(kq(qi) + causal_pen)
            logits.append((first, lg3))
        m_rep = [_sublane_allreduce(functools.reduce(jnp.maximum, [jnp.max(lg, axis=0) for lg in lgs]),
                                    jnp.maximum) for _, lgs in logits]
        outs = []
        for (first, lgs), m_u in zip(logits, m_rep):
            acc = jnp.zeros((LANE + sub, blk), F32)
            for i, lg in enumerate(lgs):
                p = jnp.exp2(lg - m_u[None])
                kv = first + i
                acc = acc + _dot(vt16[:, kv * blk:(kv + 1) * blk], p.reshape(blk, blk).astype(BF16))
            l_rep = acc[LANE:]
            outs.append((acc[0:LANE].reshape(LANE // sub, sub, blk) / l_rep[None]).reshape(LANE, blk))
        for j, qi in enumerate(group):
            out_t = jnp.where(out_row < HEAD_DIM, outs[2 * j], outs[2 * j + 1])
            o_ref[0, qi * blk:(qi + 1) * blk, :] = out_t.T.astype(o_ref.dtype)


def _moba(proj):
    b, s, _ = proj.shape
    n_pair = D_MOBA // LANE
    base = A_COLS // LANE
    spec = lambda off: pl.BlockSpec((1, s, LANE), lambda i, p: (i, 0, base + off + p))
    return pl.pallas_call(
        functools.partial(_moba_kernel, s=s),
        grid=(b, n_pair),
        in_specs=[spec(0), spec(n_pair), spec(2 * n_pair)],
        out_specs=pl.BlockSpec((1, s, LANE), lambda i, p: (i, 0, p)),
        out_shape=jax.ShapeDtypeStruct((b, s, D_MOBA), BF16),
        compiler_params=_params("parallel", "parallel"),
        name="moba",
    )(proj, proj, proj)


def _gelu(x):
    return 0.5 * x * (1.0 + jnp.tanh(0.7978845608028654 * (x + 0.044715 * (x * x * x))))


def _gmlp_kernel(u_ref, v_ref, lng_ref, lnb_ref, ws_ref, bias_ref, o_ref, *, ts):
    ch = GMLP_CHUNK
    u = _gelu(u_ref[0])
    v = _gelu(v_ref[0])
    mean = jnp.mean(v, axis=-1, keepdims=True)
    vc = v - mean
    var = jnp.mean(vc * vc, axis=-1, keepdims=True)
    vn = ((vc * lax.rsqrt(var + NORM_EPS)) * lng_ref[...] + lnb_ref[...]).astype(BF16)
    ri = lax.broadcasted_iota(jnp.int32, (ch, ch), 0)
    ci = lax.broadcasted_iota(jnp.int32, (ch, ch), 1)
    tril = ri >= ci
    w = [jnp.where(tril, ws_ref[g], 0.0).astype(BF16) for g in range(GMLP_GROUPS)]
    lane = lax.broadcasted_iota(jnp.int32, (1, D_GMLP), 1)
    bias = bias_ref[...]
    for c in range(ts // ch):
        rows = slice(c * ch, (c + 1) * ch)
        vchunk = vn[rows]
        mixed = _dot(w[0], vchunk)
        for g in range(1, GMLP_GROUPS):
            mixed = jnp.where(lane >= g * HEAD_DIM, _dot(w[g], vchunk), mixed)
        o_ref[0, rows, :] = (u[rows] * (mixed + bias)).astype(o_ref.dtype)


def _gmlp(proj, ln_g, ln_b, w_s, b_s, ts=512):
    b, s, _ = proj.shape
    base = (A_COLS + B_COLS) // D_GMLP
    bias = jnp.repeat(b_s.T, HEAD_DIM, axis=1)
    const = lambda shape: pl.BlockSpec(shape, lambda i, j: (0,) * len(shape))
    return pl.pallas_call(
        functools.partial(_gmlp_kernel, ts=ts),
        grid=(b, s // ts),
        in_specs=[
            pl.BlockSpec((1, ts, D_GMLP), lambda i, j: (i, j, base)),
            pl.BlockSpec((1, ts, D_GMLP), lambda i, j: (i, j, base + 1)),
            const((1, D_GMLP)),
            const((1, D_GMLP)),
            const((GMLP_GROUPS, GMLP_CHUNK, GMLP_CHUNK)),
            const((GMLP_CHUNK, D_GMLP)),
        ],
        out_specs=pl.BlockSpec((1, ts, D_GMLP), lambda i, j: (i, j, 0)),
        out_shape=jax.ShapeDtypeStruct((b, s, D_GMLP), BF16),
        compiler_params=_params("parallel", "parallel"),
        name="gmlp",
    )(proj, proj, ln_g.reshape(1, D_GMLP), ln_b.reshape(1, D_GMLP), w_s, bias)


def _out_proj_kernel(a_ref, b_ref, c_ref, wa_ref, wb_ref, wc_ref, x_ref, g_ref, o_ref):
    mix = (_dot(a_ref[...], wa_ref[...].astype(BF16)) + _dot(b_ref[...], wb_ref[...].astype(BF16))
           + _dot(c_ref[...], wc_ref[...].astype(BF16)))
    o_ref[...] = x_ref[...] + _rms(mix, g_ref[...])


def _out_proj(a2, b2, c2, w_all, layer, x2, g, tm=512):
    m, d = x2.shape
    row = lambda n: pl.BlockSpec((tm, n), lambda i: (i, 0))
    band = lambda rows, blk_idx: pl.BlockSpec((None, rows, d), lambda i: (layer, blk_idx, 0),
                                              pipeline_mode=pl.Buffered(1))
    return pl.pallas_call(
        _out_proj_kernel,
        grid=(m // tm,),
        in_specs=[row(D_RWKV), row(D_MOBA), row(D_GMLP),
                  band(D_RWKV, 0), band(D_MOBA, D_RWKV // D_MOBA), band(D_GMLP, (D_RWKV + D_MOBA) // D_GMLP),
                  row(d), pl.BlockSpec((1, d), lambda i: (0, 0))],
        out_specs=row(d),
        out_shape=jax.ShapeDtypeStruct((m, d), F32),
        compiler_params=_params("parallel"),
        name="out_proj",
    )(a2, b2, c2, w_all, w_all, w_all, x2, g.reshape(1, d))


def _ffn_kernel(x_ref, gpre_ref, wi_ref, wo_ref, gpost_ref, o_ref, *, tf):
    x = x_ref[...]
    h = _rms(x, gpre_ref[...]).astype(BF16)
    f = None
    for c in range(D_FF // tf):
        gate = _dot(h, wi_ref[:, c * tf:(c + 1) * tf].astype(BF16))
        up = _dot(h, wi_ref[:, D_FF + c * tf:D_FF + (c + 1) * tf].astype(BF16))
        act = (gate * _sigmoid(gate) * up).astype(BF16)
        part = _dot(act, wo_ref[c * tf:(c + 1) * tf, :].astype(BF16))
        f = part if f is None else f + part
    o_ref[...] = x + _rms(f, gpost_ref[...])


def _ffn(x2, g_pre, w_in_all, w_out_all, layer, g_post, tm=512, tf=256):
    m, d = x2.shape
    resident = lambda shape: pl.BlockSpec(shape, lambda i: (0, 0), pipeline_mode=pl.Buffered(1))
    slab = lambda shape: pl.BlockSpec((None,) + shape, lambda i: (layer, 0, 0), pipeline_mode=pl.Buffered(1))
    return pl.pallas_call(
        functools.partial(_ffn_kernel, tf=tf),
        grid=(m // tm,),
        in_specs=[
            pl.BlockSpec((tm, d), lambda i: (i, 0)),
            resident((1, d)),
            slab((d, 2 * D_FF)),
            slab((D_FF, d)),
            resident((1, d)),
        ],
        out_specs=pl.BlockSpec((tm, d), lambda i: (i, 0)),
        out_shape=jax.ShapeDtypeStruct((m, d), F32),
        compiler_params=_params("parallel"),
        name="ffn",
    )(x2, g_pre.reshape(1, d), w_in_all, w_out_all, g_post.reshape(1, d))


def kernel(x, pre_mix_g, w_in, rwkv_mu, rwkv_w0, rwkv_w2, rwkv_a0, rwkv_a2, rwkv_g2, rwkv_k_k, rwkv_k_a, rwkv_r_k, rwkv_lnx_g, rwkv_lnx_b, gmlp_ln_g, gmlp_ln_b, gmlp_w_s, gmlp_b_s, w_out, post_mix_g, pre_ffn_g, w_ffn_in, w_ffn_out, post_ffn_g):
    b, s, d = x.shape
    m = b * s
    x2 = x.reshape(m, d)
    for l in range(w_in.shape[0]):
        proj = _in_proj(x2, pre_mix_g[l], w_in, l).reshape(b, s, D_IN)
        a_out = _rwkv(proj, rwkv_mu[l], rwkv_w0[l], rwkv_w2[l], rwkv_a0[l], rwkv_a2[l], rwkv_g2[l],
                      rwkv_k_k[l], rwkv_k_a[l], rwkv_r_k[l], rwkv_lnx_g[l], rwkv_lnx_b[l])
        b_out = _moba(proj)
        c_out = _gmlp(proj, gmlp_ln_g[l], gmlp_ln_b[l], gmlp_w_s[l], gmlp_b_s[l])
        x2 = _out_proj(a_out.reshape(m, D_RWKV), b_out.reshape(m, D_MOBA), c_out.reshape(m, D_GMLP),
                       w_out, l, x2, post_mix_g[l])
        x2 = _ffn(x2, pre_ffn_g[l], w_ffn_in, w_ffn_out, l, post_ffn_g[l])
    return x2.reshape(b, s, d)
```

```python
import functools

import jax
import jax.numpy as jnp
from jax import lax
from jax.experimental import pallas as pl
from jax.experimental.pallas import tpu as pltpu

F32 = jnp.float32
BF16 = jnp.bfloat16

D_MODEL = 1024
HEAD_DIM = 64
RWKV_HEADS = 6
D_RWKV = RWKV_HEADS * HEAD_DIM
DECAY_LORA = 64
AAA_LORA = 64
GATE_LORA = 128
RWKV_LNX_EPS = 64e-5
MOBA_HEADS = 6
D_MOBA = MOBA_HEADS * HEAD_DIM
MOBA_BLOCK = 256
MOBA_TOPK = 3
GMLP_GROUPS = 4
D_GMLP = GMLP_GROUPS * HEAD_DIM
GMLP_CHUNK = 128
D_FF = 2816
NORM_EPS = 1e-6
LOG2_E = 1.4426950408889634
A_COLS = 3 * D_RWKV + DECAY_LORA + AAA_LORA + GATE_LORA
B_COLS = 3 * D_MOBA
C_COLS = 2 * D_GMLP
D_IN = A_COLS + B_COLS + C_COLS

LANE = 128
RWKV_CHUNK = 64
RWKV_GROUP_CHUNKS = 4
VMEM_LIMIT = 56 * 1024 * 1024

NN = (((1,), (0,)), ((), ()))
NT = (((1,), (1,)), ((), ()))
TN = (((0,), (0,)), ((), ()))
HI = lax.Precision.HIGHEST


def _dot(a, b, dims=NN, precision=None):
    return lax.dot_general(a, b, dims, precision=precision, preferred_element_type=F32)


def _mm(a, b, dims=NN):
    return _dot(a.astype(BF16), b.astype(BF16), dims)


def _split2(x):
    hi = x.astype(BF16)
    lo = (x - hi.astype(F32)).astype(BF16)
    return hi, lo


def _dot_exact_lhs(w_bf16, x):
    hi, lo = _split2(x)
    return _dot(w_bf16, hi) + _dot(w_bf16, lo)


def _sigmoid(x):
    return 1.0 / (1.0 + jnp.exp(-x))


def _rms(x, g):
    return x * lax.rsqrt(jnp.mean(x * x, axis=-1, keepdims=True) + NORM_EPS) * g


def _params(*sem):
    return pltpu.CompilerParams(dimension_semantics=sem, vmem_limit_bytes=VMEM_LIMIT)


def _in_proj_kernel(x_ref, g_ref, w_ref, o_ref, *, tn):
    h = _rms(x_ref[...], g_ref[...]).astype(BF16)
    for c in range(w_ref.shape[1] // tn):
        o_ref[:, c * tn:(c + 1) * tn] = _dot(h, w_ref[:, c * tn:(c + 1) * tn].astype(BF16))


def _in_proj(x2, g, w_all, layer, tm=1024, tn=512):
    m, d = x2.shape
    n = w_all.shape[2]
    return pl.pallas_call(
        functools.partial(_in_proj_kernel, tn=tn),
        grid=(m // tm,),
        in_specs=[
            pl.BlockSpec((tm, d), lambda i: (i, 0)),
            pl.BlockSpec((1, d), lambda i: (0, 0), pipeline_mode=pl.Buffered(1)),
            pl.BlockSpec((None, d, n), lambda i: (layer, 0, 0), pipeline_mode=pl.Buffered(1)),
        ],
        out_specs=pl.BlockSpec((tm, n), lambda i: (i, 0)),
        out_shape=jax.ShapeDtypeStruct((m, n), F32),
        compiler_params=_params("parallel"),
        name="in_proj",
    )(x2, g.reshape(1, d), w_all)


def _rwkv_kernel(ya_ref, mu_ref, wlo_ref, w0a0_ref, g2_ref, vecs_ref, tri_ref, hsum_ref, o_ref,
                 state_ref, prev_ref, at_ref, rt_ref, bh_ref, kh_ref, v_ref, cum_ref, y_ref, vr_ref, *, ts):
    nh, hd, c_len = RWKV_HEADS, HEAD_DIM, RWKV_CHUNK

    @pl.when(pl.program_id(1) == 0)
    def _():
        state_ref[...] = jnp.zeros_like(state_ref)
        prev_ref[...] = jnp.zeros_like(prev_ref)

    ya = ya_ref[0]
    row = lax.broadcasted_iota(jnp.int32, ya.shape, 0)
    prev = jnp.where(row == 0, prev_ref[0:1, :], pltpu.roll(ya, 1, axis=0))
    prev_ref[0:1, :] = ya[ts - 1:ts, :]
    xs = ya + mu_ref[...] * (prev - ya)

    r = xs[:, 0:D_RWKV]
    k = xs[:, D_RWKV:2 * D_RWKV]
    v = xs[:, 2 * D_RWKV:3 * D_RWKV]
    wa = xs[:, 3 * D_RWKV:3 * D_RWKV + LANE]
    gd = xs[:, 3 * D_RWKV + LANE:A_COLS]
    lane = lax.broadcasted_iota(jnp.int32, wa.shape, 1)
    wa = jnp.where(lane < DECAY_LORA, jnp.tanh(wa), wa)
    lo = _mm(wa, wlo_ref[...]) + w0a0_ref[...]
    z = -lo[:, 0:D_RWKV]
    softplus = jnp.maximum(z, 0.0) + jnp.log(1.0 + jnp.exp(-jnp.abs(z)))
    logdecay = -jnp.exp(-softplus - 0.5)
    alpha = _sigmoid(lo[:, D_RWKV:2 * D_RWKV])
    gate = _mm(_sigmoid(gd), g2_ref[...])

    k_k = vecs_ref[0:1, :]
    k_a = vecs_ref[1:2, :]
    r_k = vecs_ref[2:3, :]
    lnx_g = vecs_ref[3:4, :]
    lnx_b = vecs_ref[4:5, :]
    hsum = hsum_ref[...]

    kk = k * k_k
    kk = kk / jnp.maximum(jnp.sqrt(_mm(kk * kk, hsum)), 1e-12)
    k2 = k * (1.0 + (alpha - 1.0) * k_a)
    tri = tri_ref[...]
    tr = tri.shape[0]
    cum = jnp.concatenate([_dot_exact_lhs(tri, logdecay[i * tr:(i + 1) * tr]) for i in range(ts // tr)],
                          axis=0)
    e_neg = jnp.exp(-cum)
    at = -kk * jnp.exp(cum - logdecay)
    rt = r * jnp.exp(cum)
    bh = kk * alpha * e_neg
    kh = k2 * e_neg
    zeros_half = jnp.zeros((ts, hd), F32)
    for h in range(nh):
        sl = slice(h * hd, (h + 1) * hd)
        at_ref[h] = at[:, sl]
        rt_ref[h] = rt[:, sl]
        bh_ref[h] = bh[:, sl]
        kh_ref[h] = kh[:, sl]
        v_ref[h] = v[:, sl]
        vr_ref[h] = jnp.concatenate([zeros_half, v[:, sl]], axis=1)
        cum_ref[h] = cum[:, sl]

    ri = lax.broadcasted_iota(jnp.int32, (c_len, 2 * c_len), 0)
    ci = lax.broadcasted_iota(jnp.int32, (c_len, 2 * c_len), 1)
    ci = jnp.where(ci >= c_len, ci - c_len, ci)
    strict = ri > ci
    incl = ri >= ci
    eye = (lax.broadcasted_iota(jnp.int32, (c_len, c_len), 0)
           == lax.broadcasted_iota(jnp.int32, (c_len, c_len), 1))
    zeros_rows = jnp.zeros((c_len, hd), F32)
    n_sq = c_len.bit_length() - 1

    state = [state_ref[h] for h in range(nh)]

    def cut(ref, u):
        cc, h = u
        return ref[h, cc * c_len:(cc + 1) * c_len, :]

    for g0 in range(0, ts // c_len, RWKV_GROUP_CHUNKS):
        units = [(cc, h) for cc in range(g0, g0 + RWKV_GROUP_CHUNKS) for h in range(nh)]
        at_c = [cut(at_ref, u) for u in units]
        rt_c = [cut(rt_ref, u) for u in units]
        bh_c = [cut(bh_ref, u) for u in units]
        kh_c = [cut(kh_ref, u) for u in units]
        g_c = [jnp.exp(cum_ref[h, (cc + 1) * c_len - 1:(cc + 1) * c_len, :]) for cc, h in units]
        p_all = [_mm(jnp.concatenate([a, r_], axis=0), jnp.concatenate([b_, k_], axis=0), NT)
                 for a, r_, b_, k_ in zip(at_c, rt_c, bh_c, kh_c)]
        top = [jnp.where(strict, pa[0:c_len], 0.0) for pa in p_all]
        bot = [jnp.where(incl, pa[c_len:], 0.0) for pa in p_all]
        x2 = [_mm(t, jnp.concatenate([zeros_rows, cut(v_ref, u)], axis=0)) for t, u in zip(top, units)]
        x = [jnp.concatenate([a, b_], axis=1) for a, b_ in zip(at_c, x2)]
        p = [t[:, 0:c_len] for t in top]
        for it in range(n_sq):
            if it + 1 < n_sq:
                px = [_mm(pi, jnp.concatenate([xi, pi], axis=1)) for pi, xi in zip(p, x)]
                x = [xi + r_[:, 0:2 * hd] for xi, r_ in zip(x, px)]
                p = [r_[:, 2 * hd:] for r_ in px]
            else:
                x = [xi + _mm(pi, xi) for pi, xi in zip(p, x)]
        rhs = [jnp.concatenate([xi, cut(vr_ref, u)], axis=0) for xi, u in zip(x, units)]
        res_y = [_mm(b_, r_) for b_, r_ in zip(bot, rhs)]
        res_m = [_mm(jnp.concatenate([b_ * g, k_ * g], axis=0), r_, TN)
                 for b_, k_, g, r_ in zip(bh_c, kh_c, g_c, rhs)]
        r_eff = [r_ + ry[:, 0:hd] for r_, ry in zip(rt_c, res_y)]
        g_mat = [jnp.where(eye, g, 0.0) + rm[:, 0:hd] for g, rm in zip(g_c, res_m)]
        for i, (cc, h) in enumerate(units):
            prod = _mm(jnp.concatenate([r_eff[i], g_mat[i]], axis=0), state[h])
            y_ref[h, cc * c_len:(cc + 1) * c_len, :] = prod[0:c_len] + res_y[i][:, hd:]
            state[h] = prod[c_len:] + res_m[i][:, hd:]
    for h in range(nh):
        state_ref[h] = state[h]

    y = jnp.concatenate([y_ref[h] for h in range(nh)], axis=-1)
    inv_hd = 1.0 / hd
    mean = _mm(y, hsum) * inv_hd
    yc = y - mean
    var = _mm(yc * yc, hsum) * inv_hd
    yn = yc * lax.rsqrt(var + RWKV_LNX_EPS) * lnx_g + lnx_b
    bonus = _mm(r * k2 * r_k, hsum) * v
    o_ref[0] = ((yn + bonus) * gate).astype(o_ref.dtype)


def _rwkv(proj, mu, w0, w2, a0, a2, g2, k_k, k_a, r_k, lnx_g, lnx_b, ts=512):
    b, s, _ = proj.shape
    wlo = jnp.zeros((LANE, 2 * D_RWKV), F32)
    wlo = wlo.at[0:DECAY_LORA, 0:D_RWKV].set(w2).at[DECAY_LORA:LANE, D_RWKV:].set(a2)
    w0a0 = jnp.concatenate([w0, a0]).reshape(1, 2 * D_RWKV)
    vecs = jnp.zeros((8, D_RWKV), F32)
    vecs = vecs.at[0].set(k_k).at[1].set(k_a).at[2].set(r_k.reshape(-1)).at[3].set(lnx_g).at[4].set(lnx_b)
    tri_rows = min(ts, 256)
    ti = jnp.arange(tri_rows)
    tri = ((ti[:, None] >= ti[None, :]) & (ti[:, None] // RWKV_CHUNK == ti[None, :] // RWKV_CHUNK)).astype(BF16)
    hi_ = jnp.arange(D_RWKV) // HEAD_DIM
    hsum = (hi_[:, None] == hi_[None, :]).astype(BF16)
    const = lambda shape: pl.BlockSpec(shape, lambda i, j: (0,) * len(shape))
    head_buf = pltpu.VMEM((RWKV_HEADS, ts, HEAD_DIM), F32)
    return pl.pallas_call(
        functools.partial(_rwkv_kernel, ts=ts),
        grid=(b, s // ts),
        in_specs=[
            pl.BlockSpec((1, ts, A_COLS), lambda i, j: (i, j, 0)),
            const((1, A_COLS)),
            const((LANE, 2 * D_RWKV)),
            const((1, 2 * D_RWKV)),
            const((GATE_LORA, D_RWKV)),
            const((8, D_RWKV)),
            const((tri_rows, tri_rows)),
            const((D_RWKV, D_RWKV)),
        ],
        out_specs=pl.BlockSpec((1, ts, D_RWKV), lambda i, j: (i, j, 0)),
        out_shape=jax.ShapeDtypeStruct((b, s, D_RWKV), BF16),
        scratch_shapes=[
            pltpu.VMEM((RWKV_HEADS, HEAD_DIM, HEAD_DIM), F32),
            pltpu.VMEM((8, A_COLS), F32),
            head_buf, head_buf, head_buf, head_buf, head_buf, head_buf, head_buf,
            pltpu.VMEM((RWKV_HEADS, ts, 2 * HEAD_DIM), F32),
        ],
        compiler_params=_params("parallel", "arbitrary"),
        name="rwkv7",
    )(proj, mu.reshape(1, A_COLS), wlo, w0a0, g2, vecs, tri, hsum)


def _sublane_allreduce(x, op):
    for shift in (4, 2, 1):
        x = op(x, pltpu.roll(x, shift, axis=0))
    return x


def _moba_kernel(q_ref, k_ref, v_ref, o_ref, *, s):
    blk = MOBA_BLOCK
    nb = s // blk
    sub = 8
    q = q_ref[0]
    k = k_ref[0]
    k16 = k.astype(BF16)
    vt16 = jnp.concatenate([v_ref[0].T, jnp.ones((sub, s), F32)], axis=0).astype(BF16)
    lane = lax.broadcasted_iota(jnp.int32, (1, LANE), 1)
    kbar = jnp.concatenate(
        [jnp.mean(k[n * blk:(n + 1) * blk], axis=0, keepdims=True) for n in range(nb)], axis=0)
    key_i = lax.broadcasted_iota(jnp.int32, (blk, blk), 0)
    qry_i = lax.broadcasted_iota(jnp.int32, (blk, blk), 1)
    neg_inf = -jnp.inf
    causal_pen = jnp.where(key_i <= qry_i, 0.0, neg_inf).reshape(blk // sub, sub, blk)
    blk_id = lax.broadcasted_iota(jnp.int32, (nb, blk), 0)
    out_row = lax.broadcasted_iota(jnp.int32, (LANE, blk), 0)
    n_top = min(MOBA_TOPK, nb - 1)

    q_groups = [[qi for qi in range(nb) if qi % 4 in (0, 3)], [qi for qi in range(nb) if qi % 4 in (1, 2)]]
    for group in q_groups:
        units = [(qi, hh) for qi in group for hh in range(2)]
        logits = []
        for qi, hh in units:
            rows = slice(qi * blk, (qi + 1) * blk)
            head_lanes = (lane >= hh * HEAD_DIM) & (lane < (hh + 1) * HEAD_DIM)
            qh = jnp.where(head_lanes, q[rows], 0.0)
            qh16 = (qh * (HEAD_DIM ** -0.5 * LOG2_E)).astype(BF16)
            first = 0 if n_top > 0 else qi
            kq = lambda n: _dot(k16[n * blk:(n + 1) * blk], qh16, NT).reshape(blk // sub, sub, blk)
            lg3 = []
            if qi > first:
                sc = _dot(kbar, qh, NT, HI)
                cnt = jnp.zeros((nb, blk), F32)
                for m in range(qi):
                    cm = sc[m:m + 1, :]
                    beats = (cm > sc) | ((cm == sc) & (blk_id > m))
                    cnt = cnt + jnp.where(beats, 1.0, 0.0)
                pen = jnp.where((cnt < n_top) & (blk_id < qi), 0.0, neg_inf)
                for n in range(qi):
                    lg3.append(kq(n) + jnp.broadcast_to(pen[n:n + 1, :], (sub, blk))[None])
            lg3.append(kq(qi) + causal_pen)
            logits.append((first, lg3))
        m_rep = [_sublane_allreduce(functools.reduce(jnp.maximum, [jnp.max(lg, axis=0) for lg in lgs]),
                                    jnp.maximum) for _, lgs in logits]
        outs = []
        for (first, lgs), m_u in zip(logits, m_rep):
            acc = jnp.zeros((LANE + sub, blk), F32)
            for i, lg in enumerate(lgs):
                p = jnp.exp2(lg - m_u[None])
                kv = first + i
                acc = acc + _dot(vt16[:, kv * blk:(kv + 1) * blk], p.reshape(blk, blk).astype(BF16))
            l_rep = acc[LANE:]
            outs.append((acc[0:LANE].reshape(LANE // sub, sub, blk) / l_rep[None]).reshape(LANE, blk))
        for j, qi in enumerate(group):
            out_t = jnp.where(out_row < HEAD_DIM, outs[2 * j], outs[2 * j + 1])
            o_ref[0, qi * blk:(qi + 1) * blk, :] = out_t.T.astype(o_ref.dtype)


def _moba(proj):
    b, s, _ = proj.shape
    n_pair = D_MOBA // LANE
    base = A_COLS // LANE
    spec = lambda off: pl.BlockSpec((1, s, LANE), lambda i, p: (i, 0, base + off + p))
    return pl.pallas_call(
        functools.partial(_moba_kernel, s=s),
        grid=(b, n_pair),
        in_specs=[spec(0), spec(n_pair), spec(2 * n_pair)],
        out_specs=pl.BlockSpec((1, s, LANE), lambda i, p: (i, 0, p)),
        out_shape=jax.ShapeDtypeStruct((b, s, D_MOBA), BF16),
        compiler_params=_params("parallel", "parallel"),
        name="moba",
    )(proj, proj, proj)


def _gelu(x):
    return 0.5 * x * (1.0 + jnp.tanh(0.7978845608028654 * (x + 0.044715 * (x * x * x))))


def _gmlp_kernel(u_ref, v_ref, lng_ref, lnb_ref, ws_ref, bias_ref, o_ref, *, ts):
    ch = GMLP_CHUNK
    u = _gelu(u_ref[0])
    v = _gelu(v_ref[0])
    mean = jnp.mean(v, axis=-1, keepdims=True)
    vc = v - mean
    var = jnp.mean(vc * vc, axis=-1, keepdims=True)
    vn = ((vc * lax.rsqrt(var + NORM_EPS)) * lng_ref[...] + lnb_ref[...]).astype(BF16)
    ri = lax.broadcasted_iota(jnp.int32, (ch, ch), 0)
    ci = lax.broadcasted_iota(jnp.int32, (ch, ch), 1)
    tril = ri >= ci
    w = [jnp.where(tril, ws_ref[g], 0.0).astype(BF16) for g in range(GMLP_GROUPS)]
    lane = lax.broadcasted_iota(jnp.int32, (1, D_GMLP), 1)
    bias = bias_ref[...]
    for c in range(ts // ch):
        rows = slice(c * ch, (c + 1) * ch)
        vchunk = vn[rows]
        mixed = _dot(w[0], vchunk)
        for g in range(1, GMLP_GROUPS):
            mixed = jnp.where(lane >= g * HEAD_DIM, _dot(w[g], vchunk), mixed)
        o_ref[0, rows, :] = (u[rows] * (mixed + bias)).astype(o_ref.dtype)


def _gmlp(proj, ln_g, ln_b, w_s, b_s, ts=1024):
    b, s, _ = proj.shape
    base = (A_COLS + B_COLS) // D_GMLP
    bias = jnp.repeat(b_s.T, HEAD_DIM, axis=1)
    const = lambda shape: pl.BlockSpec(shape, lambda i, j: (0,) * len(shape))
    return pl.pallas_call(
        functools.partial(_gmlp_kernel, ts=ts),
        grid=(b, s // ts),
        in_specs=[
            pl.BlockSpec((1, ts, D_GMLP), lambda i, j: (i, j, base)),
            pl.BlockSpec((1, ts, D_GMLP), lambda i, j: (i, j, base + 1)),
            const((1, D_GMLP)),
            const((1, D_GMLP)),
            const((GMLP_GROUPS, GMLP_CHUNK, GMLP_CHUNK)),
            const((GMLP_CHUNK, D_GMLP)),
        ],
        out_specs=pl.BlockSpec((1, ts, D_GMLP), lambda i, j: (i, j, 0)),
        out_shape=jax.ShapeDtypeStruct((b, s, D_GMLP), BF16),
        compiler_params=_params("parallel", "parallel"),
        name="gmlp",
    )(proj, proj, ln_g.reshape(1, D_GMLP), ln_b.reshape(1, D_GMLP), w_s, bias)


def _out_proj_kernel(a_ref, b_ref, c_ref, w_ref, x_ref, g_ref, o_ref):
    mixed_in = jnp.concatenate([a_ref[...], b_ref[...], c_ref[...]], axis=-1)
    mix = _dot(mixed_in, w_ref[...].astype(BF16))
    o_ref[...] = x_ref[...] + _rms(mix, g_ref[...])


def _out_proj(a2, b2, c2, w_all, layer, x2, g, tm=1024):
    m, d = x2.shape
    row = lambda n: pl.BlockSpec((tm, n), lambda i: (i, 0))
    return pl.pallas_call(
        _out_proj_kernel,
        grid=(m // tm,),
        in_specs=[row(D_RWKV), row(D_MOBA), row(D_GMLP),
                  pl.BlockSpec((None, D_MODEL, d), lambda i: (layer, 0, 0), pipeline_mode=pl.Buffered(1)),
                  row(d), pl.BlockSpec((1, d), lambda i: (0, 0))],
        out_specs=row(d),
        out_shape=jax.ShapeDtypeStruct((m, d), F32),
        compiler_params=_params("parallel"),
        name="out_proj",
    )(a2, b2, c2, w_all, x2, g.reshape(1, d))


def _ffn_kernel(x_ref, gpre_ref, wi_ref, wo_ref, gpost_ref, o_ref, *, tf):
    x = x_ref[...]
    h = _rms(x, gpre_ref[...]).astype(BF16)
    f = None
    for c in range(D_FF // tf):
        gate = _dot(h, wi_ref[:, c * tf:(c + 1) * tf].astype(BF16))
        up = _dot(h, wi_ref[:, D_FF + c * tf:D_FF + (c + 1) * tf].astype(BF16))
        act = (gate * _sigmoid(gate) * up).astype(BF16)
        part = _dot(act, wo_ref[c * tf:(c + 1) * tf, :].astype(BF16))
        f = part if f is None else f + part
    o_ref[...] = x + _rms(f, gpost_ref[...])


def _ffn(x2, g_pre, w_in_all, w_out_all, layer, g_post, tm=512, tf=256):
    m, d = x2.shape
    resident = lambda shape: pl.BlockSpec(shape, lambda i: (0, 0), pipeline_mode=pl.Buffered(1))
    slab = lambda shape: pl.BlockSpec((None,) + shape, lambda i: (layer, 0, 0), pipeline_mode=pl.Buffered(1))
    return pl.pallas_call(
        functools.partial(_ffn_kernel, tf=tf),
        grid=(m // tm,),
        in_specs=[
            pl.BlockSpec((tm, d), lambda i: (i, 0)),
            resident((1, d)),
            slab((d, 2 * D_FF)),
            slab((D_FF, d)),
            resident((1, d)),
        ],
        out_specs=pl.BlockSpec((tm, d), lambda i: (i, 0)),
        out_shape=jax.ShapeDtypeStruct((m, d), F32),
        compiler_params=_params("parallel"),
        name="ffn",
    )(x2, g_pre.reshape(1, d), w_in_all, w_out_all, g_post.reshape(1, d))


def kernel(x, pre_mix_g, w_in, rwkv_mu, rwkv_w0, rwkv_w2, rwkv_a0, rwkv_a2, rwkv_g2, rwkv_k_k, rwkv_k_a, rwkv_r_k, rwkv_lnx_g, rwkv_lnx_b, gmlp_ln_g, gmlp_ln_b, gmlp_w_s, gmlp_b_s, w_out, post_mix_g, pre_ffn_g, w_ffn_in, w_ffn_out, post_ffn_g):
    b, s, d = x.shape
    m = b * s
    x2 = x.reshape(m, d)
    for l in range(w_in.shape[0]):
        proj = _in_proj(x2, pre_mix_g[l], w_in, l).reshape(b, s, D_IN)
        a_out = _rwkv(proj, rwkv_mu[l], rwkv_w0[l], rwkv_w2[l], rwkv_a0[l], rwkv_a2[l], rwkv_g2[l],
                      rwkv_k_k[l], rwkv_k_a[l], rwkv_r_k[l], rwkv_lnx_g[l], rwkv_lnx_b[l])
        b_out = _moba(proj)
        c_out = _gmlp(proj, gmlp_ln_g[l], gmlp_ln_b[l], gmlp_w_s[l], gmlp_b_s[l])
        x2 = _out_proj(a_out.reshape(m, D_RWKV), b_out.reshape(m, D_MOBA), c_out.reshape(m, D_GMLP),
                       w_out, l, x2, post_mix_g[l])
        x2 = _ffn(x2, pre_ffn_g[l], w_ffn_in, w_ffn_out, l, post_ffn_g[l])
    return x2.reshape(b, s, d)
```

```python
import functools

import jax
import jax.numpy as jnp
from jax import lax
from jax.experimental import pallas as pl
from jax.experimental.pallas import tpu as pltpu

F32 = jnp.float32
BF16 = jnp.bfloat16

D_MODEL = 1024
HEAD_DIM = 64
RWKV_HEADS = 6
D_RWKV = RWKV_HEADS * HEAD_DIM
DECAY_LORA = 64
AAA_LORA = 64
GATE_LORA = 128
RWKV_LNX_EPS = 64e-5
MOBA_HEADS = 6
D_MOBA = MOBA_HEADS * HEAD_DIM
MOBA_BLOCK = 256
MOBA_TOPK = 3
GMLP_GROUPS = 4
D_GMLP = GMLP_GROUPS * HEAD_DIM
GMLP_CHUNK = 128
D_FF = 2816
NORM_EPS = 1e-6
LOG2_E = 1.4426950408889634
A_COLS = 3 * D_RWKV + DECAY_LORA + AAA_LORA + GATE_LORA
B_COLS = 3 * D_MOBA
C_COLS = 2 * D_GMLP
D_IN = A_COLS + B_COLS + C_COLS

LANE = 128
SUBLANE = 8
MXU_DEPTH = 256
RWKV_CHUNK = 64
RWKV_GROUP_CHUNKS = 4
VMEM_LIMIT = 56 * 1024 * 1024

NN = (((1,), (0,)), ((), ()))
NT = (((1,), (1,)), ((), ()))
TN = (((0,), (0,)), ((), ()))
HI = lax.Precision.HIGHEST


def _dot(a, b, dims=NN, precision=None):
    return lax.dot_general(a, b, dims, precision=precision, preferred_element_type=F32)


def _mm(a, b, dims=NN):
    return _dot(a.astype(BF16), b.astype(BF16), dims)


def _split2(x):
    hi = x.astype(BF16)
    lo = (x - hi.astype(F32)).astype(BF16)
    return hi, lo


def _dot_exact_lhs(w_bf16, x):
    hi, lo = _split2(x)
    return _dot(w_bf16, hi) + _dot(w_bf16, lo)


def _sigmoid(x):
    return 1.0 / (1.0 + jnp.exp(-x))


def _rms(x, g):
    return x * lax.rsqrt(jnp.mean(x * x, axis=-1, keepdims=True) + NORM_EPS) * g


def _params(*sem):
    return pltpu.CompilerParams(dimension_semantics=sem, vmem_limit_bytes=VMEM_LIMIT)


def _in_proj_kernel(x_ref, g_ref, w_ref, o_ref, *, tn):
    h = _rms(x_ref[...], g_ref[...]).astype(BF16)
    for c in range(w_ref.shape[1] // tn):
        o_ref[:, c * tn:(c + 1) * tn] = _dot(h, w_ref[:, c * tn:(c + 1) * tn].astype(BF16))


def _in_proj(x2, g, w_all, layer, tm=1024, tn=512):
    m, d = x2.shape
    n = w_all.shape[2]
    return pl.pallas_call(
        functools.partial(_in_proj_kernel, tn=tn),
        grid=(m // tm,),
        in_specs=[
            pl.BlockSpec((tm, d), lambda i: (i, 0)),
            pl.BlockSpec((1, d), lambda i: (0, 0), pipeline_mode=pl.Buffered(1)),
            pl.BlockSpec((None, d, n), lambda i: (layer, 0, 0), pipeline_mode=pl.Buffered(1)),
        ],
        out_specs=pl.BlockSpec((tm, n), lambda i: (i, 0)),
        out_shape=jax.ShapeDtypeStruct((m, n), F32),
        compiler_params=_params("parallel"),
        name="in_proj",
    )(x2, g.reshape(1, d), w_all)


def _rwkv_kernel(ya_ref, mu_ref, wlo_ref, w0a0_ref, g2_ref, vecs_ref, tri_ref, hsum_ref, o_ref,
                 state_ref, prev_ref, at_ref, rt_ref, bh_ref, kh_ref, v_ref, cum_ref, y_ref, vr_ref, *, ts):
    nh, hd, c_len = RWKV_HEADS, HEAD_DIM, RWKV_CHUNK

    @pl.when(pl.program_id(1) == 0)
    def _():
        state_ref[...] = jnp.zeros_like(state_ref)
        prev_ref[...] = jnp.zeros_like(prev_ref)

    ya = ya_ref[0]
    row = lax.broadcasted_iota(jnp.int32, ya.shape, 0)
    prev = jnp.where(row == 0, prev_ref[0:1, :], pltpu.roll(ya, 1, axis=0))
    prev_ref[0:1, :] = ya[ts - 1:ts, :]
    xs = ya + mu_ref[...] * (prev - ya)

    r = xs[:, 0:D_RWKV]
    k = xs[:, D_RWKV:2 * D_RWKV]
    v = xs[:, 2 * D_RWKV:3 * D_RWKV]
    wa = xs[:, 3 * D_RWKV:3 * D_RWKV + LANE]
    gd = xs[:, 3 * D_RWKV + LANE:A_COLS]
    lane = lax.broadcasted_iota(jnp.int32, wa.shape, 1)
    wa = jnp.where(lane < DECAY_LORA, jnp.tanh(wa), wa)
    lo = _mm(wa, wlo_ref[...]) + w0a0_ref[...]
    z = -lo[:, 0:D_RWKV]
    softplus = jnp.maximum(z, 0.0) + jnp.log(1.0 + jnp.exp(-jnp.abs(z)))
    logdecay = -jnp.exp(-softplus - 0.5)
    alpha = _sigmoid(lo[:, D_RWKV:2 * D_RWKV])
    gate = _mm(_sigmoid(gd), g2_ref[...])

    k_k = vecs_ref[0:1, :]
    k_a = vecs_ref[1:2, :]
    r_k = vecs_ref[2:3, :]
    lnx_g = vecs_ref[3:4, :]
    lnx_b = vecs_ref[4:5, :]
    hsum = hsum_ref[...]

    kk = k * k_k
    kk = kk / jnp.maximum(jnp.sqrt(_mm(kk * kk, hsum)), 1e-12)
    k2 = k * (1.0 + (alpha - 1.0) * k_a)
    tri = tri_ref[...]
    tr = tri.shape[0]
    cum = jnp.concatenate([_dot_exact_lhs(tri, logdecay[i * tr:(i + 1) * tr]) for i in range(ts // tr)],
                          axis=0)
    e_neg = jnp.exp(-cum)
    at = -kk * jnp.exp(cum - logdecay)
    rt = r * jnp.exp(cum)
    bh = kk * alpha * e_neg
    kh = k2 * e_neg
    zeros_half = jnp.zeros((ts, hd), F32)
    for h in range(nh):
        sl = slice(h * hd, (h + 1) * hd)
        at_ref[h] = at[:, sl]
        rt_ref[h] = rt[:, sl]
        bh_ref[h] = bh[:, sl]
        kh_ref[h] = kh[:, sl]
        v_ref[h] = v[:, sl]
        vr_ref[h] = jnp.concatenate([zeros_half, v[:, sl]], axis=1)
        cum_ref[h] = cum[:, sl]

    ri = lax.broadcasted_iota(jnp.int32, (c_len, 2 * c_len), 0)
    ci = lax.broadcasted_iota(jnp.int32, (c_len, 2 * c_len), 1)
    ci = jnp.where(ci >= c_len, ci - c_len, ci)
    strict = ri > ci
    incl = ri >= ci
    eye = (lax.broadcasted_iota(jnp.int32, (c_len, c_len), 0)
           == lax.broadcasted_iota(jnp.int32, (c_len, c_len), 1))
    zeros_rows = jnp.zeros((c_len, hd), F32)
    n_sq = c_len.bit_length() - 1

    state = [state_ref[h] for h in range(nh)]

    def cut(ref, u):
        cc, h = u
        return ref[h, cc * c_len:(cc + 1) * c_len, :]

    for g0 in range(0, ts // c_len, RWKV_GROUP_CHUNKS):
        units = [(cc, h) for cc in range(g0, g0 + RWKV_GROUP_CHUNKS) for h in range(nh)]
        at_c = [cut(at_ref, u) for u in units]
        rt_c = [cut(rt_ref, u) for u in units]
        bh_c = [cut(bh_ref, u) for u in units]
        kh_c = [cut(kh_ref, u) for u in units]
        g_c = [jnp.exp(cum_ref[h, (cc + 1) * c_len - 1:(cc + 1) * c_len, :]) for cc, h in units]
        p_all = [_mm(jnp.concatenate([a, r_], axis=0), jnp.concatenate([b_, k_], axis=0), NT)
                 for a, r_, b_, k_ in zip(at_c, rt_c, bh_c, kh_c)]
        top = [jnp.where(strict, pa[0:c_len], 0.0) for pa in p_all]
        bot = [jnp.where(incl, pa[c_len:], 0.0) for pa in p_all]
        x2 = [_mm(t, jnp.concatenate([zeros_rows, cut(v_ref, u)], axis=0)) for t, u in zip(top, units)]
        x = [jnp.concatenate([a, b_], axis=1) for a, b_ in zip(at_c, x2)]
        p = [t[:, 0:c_len] for t in top]
        for it in range(n_sq):
            if it + 1 < n_sq:
                px = [_mm(pi, jnp.concatenate([xi, pi], axis=1)) for pi, xi in zip(p, x)]
                x = [xi + r_[:, 0:2 * hd] for xi, r_ in zip(x, px)]
                p = [r_[:, 2 * hd:] for r_ in px]
            else:
                x = [xi + _mm(pi, xi) for pi, xi in zip(p, x)]
        rhs = [jnp.concatenate([xi, cut(vr_ref, u)], axis=0) for xi, u in zip(x, units)]
        res_y = [_mm(b_, r_) for b_, r_ in zip(bot, rhs)]
        res_m = [_mm(jnp.concatenate([b_ * g, k_ * g], axis=0), r_, TN)
                 for b_, k_, g, r_ in zip(bh_c, kh_c, g_c, rhs)]
        r_eff = [r_ + ry[:, 0:hd] for r_, ry in zip(rt_c, res_y)]
        g_mat = [jnp.where(eye, g, 0.0) + rm[:, 0:hd] for g, rm in zip(g_c, res_m)]
        for i, (cc, h) in enumerate(units):
            prod = _mm(jnp.concatenate([r_eff[i], g_mat[i]], axis=0), state[h])
            y_ref[h, cc * c_len:(cc + 1) * c_len, :] = prod[0:c_len] + res_y[i][:, hd:]
            state[h] = prod[c_len:] + res_m[i][:, hd:]
    for h in range(nh):
        state_ref[h] = state[h]

    y = jnp.concatenate([y_ref[h] for h in range(nh)], axis=-1)
    inv_hd = 1.0 / hd
    mean = _mm(y, hsum) * inv_hd
    yc = y - mean
    var = _mm(yc * yc, hsum) * inv_hd
    yn = yc * lax.rsqrt(var + RWKV_LNX_EPS) * lnx_g + lnx_b
    bonus = _mm(r * k2 * r_k, hsum) * v
    o_ref[0] = ((yn + bonus) * gate).astype(o_ref.dtype)


def _rwkv(proj, mu, w0, w2, a0, a2, g2, k_k, k_a, r_k, lnx_g, lnx_b, ts=512):
    b, s, _ = proj.shape
    wlo = jnp.zeros((LANE, 2 * D_RWKV), F32)
    wlo = wlo.at[0:DECAY_LORA, 0:D_RWKV].set(w2).at[DECAY_LORA:LANE, D_RWKV:].set(a2)
    w0a0 = jnp.concatenate([w0, a0]).reshape(1, 2 * D_RWKV)
    vecs = jnp.zeros((SUBLANE, D_RWKV), F32)
    vecs = vecs.at[0].set(k_k).at[1].set(k_a).at[2].set(r_k.reshape(-1)).at[3].set(lnx_g).at[4].set(lnx_b)
    tri_rows = min(ts, MXU_DEPTH)
    ti = jnp.arange(tri_rows)
    tri = ((ti[:, None] >= ti[None, :]) & (ti[:, None] // RWKV_CHUNK == ti[None, :] // RWKV_CHUNK)).astype(BF16)
    hi_ = jnp.arange(D_RWKV) // HEAD_DIM
    hsum = (hi_[:, None] == hi_[None, :]).astype(BF16)
    const = lambda shape: pl.BlockSpec(shape, lambda i, j: (0,) * len(shape))
    head_buf = pltpu.VMEM((RWKV_HEADS, ts, HEAD_DIM), F32)
    return pl.pallas_call(
        functools.partial(_rwkv_kernel, ts=ts),
        grid=(b, s // ts),
        in_specs=[
            pl.BlockSpec((1, ts, A_COLS), lambda i, j: (i, j, 0)),
            const((1, A_COLS)),
            const((LANE, 2 * D_RWKV)),
            const((1, 2 * D_RWKV)),
            const((GATE_LORA, D_RWKV)),
            const((SUBLANE, D_RWKV)),
            const((tri_rows, tri_rows)),
            const((D_RWKV, D_RWKV)),
        ],
        out_specs=pl.BlockSpec((1, ts, D_RWKV), lambda i, j: (i, j, 0)),
        out_shape=jax.ShapeDtypeStruct((b, s, D_RWKV), BF16),
        scratch_shapes=[
            pltpu.VMEM((RWKV_HEADS, HEAD_DIM, HEAD_DIM), F32),
            pltpu.VMEM((SUBLANE, A_COLS), F32),
            head_buf, head_buf, head_buf, head_buf, head_buf, head_buf, head_buf,
            pltpu.VMEM((RWKV_HEADS, ts, 2 * HEAD_DIM), F32),
        ],
        compiler_params=_params("parallel", "arbitrary"),
        name="rwkv7",
    )(proj, mu.reshape(1, A_COLS), wlo, w0a0, g2, vecs, tri, hsum)


def _sublane_allreduce(x, op):
    shift = SUBLANE // 2
    while shift:
        x = op(x, pltpu.roll(x, shift, axis=0))
        shift //= 2
    return x


def _moba_kernel(q_ref, k_ref, v_ref, o_ref, *, s):
    blk = MOBA_BLOCK
    nb = s // blk
    sub = SUBLANE
    q = q_ref[0]
    k = k_ref[0]
    k16 = k.astype(BF16)
    vt16 = jnp.concatenate([v_ref[0].T, jnp.ones((sub, s), F32)], axis=0).astype(BF16)
    lane = lax.broadcasted_iota(jnp.int32, (1, LANE), 1)
    kbar = jnp.concatenate(
        [jnp.mean(k[n * blk:(n + 1) * blk], axis=0, keepdims=True) for n in range(nb)], axis=0)
    key_i = lax.broadcasted_iota(jnp.int32, (blk, blk), 0)
    qry_i = lax.broadcasted_iota(jnp.int32, (blk, blk), 1)
    neg_inf = -jnp.inf
    causal_pen = jnp.where(key_i <= qry_i, 0.0, neg_inf).reshape(blk // sub, sub, blk)
    blk_id = lax.broadcasted_iota(jnp.int32, (nb, blk), 0)
    out_row = lax.broadcasted_iota(jnp.int32, (LANE, blk), 0)
    n_top = min(MOBA_TOPK, nb - 1)

    q_groups = [[qi for qi in range(nb) if qi % 4 in (0, 3)], [qi for qi in range(nb) if qi % 4 in (1, 2)]]
    for group in q_groups:
        units = [(qi, hh) for qi in group for hh in range(2)]
        logits = []
        for qi, hh in units:
            rows = slice(qi * blk, (qi + 1) * blk)
            head_lanes = (lane >= hh * HEAD_DIM) & (lane < (hh + 1) * HEAD_DIM)
            qh = jnp.where(head_lanes, q[rows], 0.0)
            qh16 = (qh * (HEAD_DIM ** -0.5 * LOG2_E)).astype(BF16)
            first = 0 if n_top > 0 else qi
            kq = lambda n: _dot(k16[n * blk:(n + 1) * blk], qh16, NT).reshape(blk // sub, sub, blk)
            lg3 = []
            if qi > first:
                sc = _dot(kbar, qh, NT, HI)
                cnt = jnp.zeros((nb, blk), F32)
                for m in range(qi):
                    cm = sc[m:m + 1, :]
                    beats = (cm > sc) | ((cm == sc) & (blk_id > m))
                    cnt = cnt + jnp.where(beats, 1.0, 0.0)
                pen = jnp.where((cnt < n_top) & (blk_id < qi), 0.0, neg_inf)
                for n in range(qi):
                    lg3.append(kq(n) + jnp.broadcast_to(pen[n:n + 1, :], (sub, blk))[None])
            lg3.append(kq(qi) + causal_pen)
            logits.append((first, lg3))
        m_rep = [_sublane_allreduce(functools.reduce(jnp.maximum, [jnp.max(lg, axis=0) for lg in lgs]),
                                    jnp.maximum) for _, lgs in logits]
        outs = []
        for (first, lgs), m_u in zip(logits, m_rep):
            acc = jnp.zeros((LANE + sub, blk), F32)
            for i, lg in enumerate(lgs):
                p = jnp.exp2(lg - m_u[None])
                kv = first + i
                acc = acc + _dot(vt16[:, kv * blk:(kv + 1) * blk], p.reshape(blk, blk).astype(BF16))
            l_rep = acc[LANE:]
            outs.append((acc[0:LANE].reshape(LANE // sub, sub, blk) / l_rep[None]).reshape(LANE, blk))
        for j, qi in enumerate(group):
            out_t = jnp.where(out_row < HEAD_DIM, outs[2 * j], outs[2 * j + 1])
            o_ref[0, qi * blk:(qi + 1) * blk, :] = out_t.T.astype(o_ref.dtype)


def _moba(proj):
    b, s, _ = proj.shape
    n_pair = D_MOBA // LANE
    base = A_COLS // LANE
    spec = lambda off: pl.BlockSpec((1, s, LANE), lambda i, p: (i, 0, base + off + p))
    return pl.pallas_call(
        functools.partial(_moba_kernel, s=s),
        grid=(b, n_pair),
        in_specs=[spec(0), spec(n_pair), spec(2 * n_pair)],
        out_specs=pl.BlockSpec((1, s, LANE), lambda i, p: (i, 0, p)),
        out_shape=jax.ShapeDtypeStruct((b, s, D_MOBA), BF16),
        compiler_params=_params("parallel", "parallel"),
        name="moba",
    )(proj, proj, proj)


def _gelu(x):
    return 0.5 * x * (1.0 + jnp.tanh(0.7978845608028654 * (x + 0.044715 * (x * x * x))))


def _gmlp_kernel(u_ref, v_ref, lng_ref, lnb_ref, ws_ref, bias_ref, o_ref, *, ts):
    ch = GMLP_CHUNK
    u = _gelu(u_ref[0])
    v = _gelu(v_ref[0])
    mean = jnp.mean(v, axis=-1, keepdims=True)
    vc = v - mean
    var = jnp.mean(vc * vc, axis=-1, keepdims=True)
    vn = ((vc * lax.rsqrt(var + NORM_EPS)) * lng_ref[...] + lnb_ref[...]).astype(BF16)
    ri = lax.broadcasted_iota(jnp.int32, (ch, ch), 0)
    ci = lax.broadcasted_iota(jnp.int32, (ch, ch), 1)
    tril = ri >= ci
    w = [jnp.where(tril, ws_ref[g], 0.0).astype(BF16) for g in range(GMLP_GROUPS)]
    lane = lax.broadcasted_iota(jnp.int32, (1, D_GMLP), 1)
    bias = bias_ref[...]
    for c in range(ts // ch):
        rows = slice(c * ch, (c + 1) * ch)
        vchunk = vn[rows]
        mixed = _dot(w[0], vchunk)
        for g in range(1, GMLP_GROUPS):
            mixed = jnp.where(lane >= g * HEAD_DIM, _dot(w[g], vchunk), mixed)
        o_ref[0, rows, :] = (u[rows] * (mixed + bias)).astype(o_ref.dtype)


def _gmlp(proj, ln_g, ln_b, w_s, b_s, ts=1024):
    b, s, _ = proj.shape
    base = (A_COLS + B_COLS) // D_GMLP
    bias = jnp.repeat(b_s.T, HEAD_DIM, axis=1)
    const = lambda shape: pl.BlockSpec(shape, lambda i, j: (0,) * len(shape))
    return pl.pallas_call(
        functools.partial(_gmlp_kernel, ts=ts),
        grid=(b, s // ts),
        in_specs=[
            pl.BlockSpec((1, ts, D_GMLP), lambda i, j: (i, j, base)),
            pl.BlockSpec((1, ts, D_GMLP), lambda i, j: (i, j, base + 1)),
            const((1, D_GMLP)),
            const((1, D_GMLP)),
            const((GMLP_GROUPS, GMLP_CHUNK, GMLP_CHUNK)),
            const((GMLP_CHUNK, D_GMLP)),
        ],
        out_specs=pl.BlockSpec((1, ts, D_GMLP), lambda i, j: (i, j, 0)),
        out_shape=jax.ShapeDtypeStruct((b, s, D_GMLP), BF16),
        compiler_params=_params("parallel", "parallel"),
        name="gmlp",
    )(proj, proj, ln_g.reshape(1, D_GMLP), ln_b.reshape(1, D_GMLP), w_s, bias)


def _out_proj_kernel(a_ref, b_ref, c_ref, w_ref, x_ref, g_ref, o_ref):
    mixed_in = jnp.concatenate([a_ref[...], b_ref[...], c_ref[...]], axis=-1)
    mix = _dot(mixed_in, w_ref[...].astype(BF16))
    o_ref[...] = x_ref[...] + _rms(mix, g_ref[...])


def _out_proj(a2, b2, c2, w_all, layer, x2, g, tm=1024):
    m, d = x2.shape
    row = lambda n: pl.BlockSpec((tm, n), lambda i: (i, 0))
    return pl.pallas_call(
        _out_proj_kernel,
        grid=(m // tm,),
        in_specs=[row(D_RWKV), row(D_MOBA), row(D_GMLP),
                  pl.BlockSpec((None, D_MODEL, d), lambda i: (layer, 0, 0), pipeline_mode=pl.Buffered(1)),
                  row(d), pl.BlockSpec((1, d), lambda i: (0, 0))],
        out_specs=row(d),
        out_shape=jax.ShapeDtypeStruct((m, d), F32),
        compiler_params=_params("parallel"),
        name="out_proj",
    )(a2, b2, c2, w_all, x2, g.reshape(1, d))


def _ffn_kernel(x_ref, gpre_ref, wi_ref, wo_ref, gpost_ref, o_ref, *, tf):
    x = x_ref[...]
    h = _rms(x, gpre_ref[...]).astype(BF16)
    f = None
    for c in range(D_FF // tf):
        gate = _dot(h, wi_ref[:, c * tf:(c + 1) * tf].astype(BF16))
        up = _dot(h, wi_ref[:, D_FF + c * tf:D_FF + (c + 1) * tf].astype(BF16))
        act = (gate * _sigmoid(gate) * up).astype(BF16)
        part = _dot(act, wo_ref[c * tf:(c + 1) * tf, :].astype(BF16))
        f = part if f is None else f + part
    o_ref[...] = x + _rms(f, gpost_ref[...])


def _ffn(x2, g_pre, w_in_all, w_out_all, layer, g_post, tm=512, tf=256):
    m, d = x2.shape
    resident = lambda shape: pl.BlockSpec(shape, lambda i: (0, 0), pipeline_mode=pl.Buffered(1))
    slab = lambda shape: pl.BlockSpec((None,) + shape, lambda i: (layer, 0, 0), pipeline_mode=pl.Buffered(1))
    return pl.pallas_call(
        functools.partial(_ffn_kernel, tf=tf),
        grid=(m // tm,),
        in_specs=[
            pl.BlockSpec((tm, d), lambda i: (i, 0)),
            resident((1, d)),
            slab((d, 2 * D_FF)),
            slab((D_FF, d)),
            resident((1, d)),
        ],
        out_specs=pl.BlockSpec((tm, d), lambda i: (i, 0)),
        out_shape=jax.ShapeDtypeStruct((m, d), F32),
        compiler_params=_params("parallel"),
        name="ffn",
    )(x2, g_pre.reshape(1, d), w_in_all, w_out_all, g_post.reshape(1, d))


def kernel(x, pre_mix_g, w_in, rwkv_mu, rwkv_w0, rwkv_w2, rwkv_a0, rwkv_a2, rwkv_g2, rwkv_k_k, rwkv_k_a, rwkv_r_k, rwkv_lnx_g, rwkv_lnx_b, gmlp_ln_g, gmlp_ln_b, gmlp_w_s, gmlp_b_s, w_out, post_mix_g, pre_ffn_g, w_ffn_in, w_ffn_out, post_ffn_g):
    b, s, d = x.shape
    m = b * s
    x2 = x.reshape(m, d)
    for l in range(w_in.shape[0]):
        proj = _in_proj(x2, pre_mix_g[l], w_in, l).reshape(b, s, D_IN)
        a_out = _rwkv(proj, rwkv_mu[l], rwkv_w0[l], rwkv_w2[l], rwkv_a0[l], rwkv_a2[l], rwkv_g2[l],
                      rwkv_k_k[l], rwkv_k_a[l], rwkv_r_k[l], rwkv_lnx_g[l], rwkv_lnx_b[l])
        b_out = _moba(proj)
        c_out = _gmlp(proj, gmlp_ln_g[l], gmlp_ln_b[l], gmlp_w_s[l], gmlp_b_s[l])
        x2 = _out_proj(a_out.reshape(m, D_RWKV), b_out.reshape(m, D_MOBA), c_out.reshape(m, D_GMLP),
                       w_out, l, x2, post_mix_g[l])
        x2 = _ffn(x2, pre_ffn_g[l], w_ffn_in, w_ffn_out, l, post_ffn_g[l])
    return x2.reshape(b, s, d)
```

```python
import functools

import jax
import jax.numpy as jnp
from jax import lax
from jax.experimental import pallas as pl
from jax.experimental.pallas import tpu as pltpu

F32 = jnp.float32
BF16 = jnp.bfloat16

D_MODEL = 1024
HEAD_DIM = 64
RWKV_HEADS = 6
D_RWKV = RWKV_HEADS * HEAD_DIM
DECAY_LORA = 64
AAA_LORA = 64
GATE_LORA = 128
RWKV_LNX_EPS = 64e-5
MOBA_HEADS = 6
D_MOBA = MOBA_HEADS * HEAD_DIM
MOBA_BLOCK = 256
MOBA_TOPK = 3
GMLP_GROUPS = 4
D_GMLP = GMLP_GROUPS * HEAD_DIM
GMLP_CHUNK = 128
D_FF = 2816
NORM_EPS = 1e-6
LOG2_E = 1.4426950408889634
A_COLS = 3 * D_RWKV + DECAY_LORA + AAA_LORA + GATE_LORA
B_COLS = 3 * D_MOBA
C_COLS = 2 * D_GMLP
D_IN = A_COLS + B_COLS + C_COLS

LANE = 128
SUBLANE = 8
MXU_DEPTH = 256
RWKV_CHUNK = 64
RWKV_GROUP_CHUNKS = 4
VMEM_LIMIT = 56 * 1024 * 1024

NN = (((1,), (0,)), ((), ()))
NT = (((1,), (1,)), ((), ()))
TN = (((0,), (0,)), ((), ()))
HI = lax.Precision.HIGHEST


def _dot(a, b, dims=NN, precision=None):
    return lax.dot_general(a, b, dims, precision=precision, preferred_element_type=F32)


def _mm(a, b, dims=NN):
    return _dot(a.astype(BF16), b.astype(BF16), dims)


def _split2(x):
    hi = x.astype(BF16)
    lo = (x - hi.astype(F32)).astype(BF16)
    return hi, lo


def _dot_exact_lhs(w_bf16, x):
    hi, lo = _split2(x)
    return _dot(w_bf16, hi) + _dot(w_bf16, lo)


def _sigmoid(x):
    return 1.0 / (1.0 + jnp.exp(-x))


def _rms(x, g):
    return x * lax.rsqrt(jnp.mean(x * x, axis=-1, keepdims=True) + NORM_EPS) * g


def _params(*sem):
    return pltpu.CompilerParams(dimension_semantics=sem, vmem_limit_bytes=VMEM_LIMIT)


def _in_proj_kernel(x_ref, g_ref, w_ref, o_ref, *, tn):
    h = _rms(x_ref[...], g_ref[...]).astype(BF16)
    for c in range(w_ref.shape[1] // tn):
        o_ref[:, c * tn:(c + 1) * tn] = _dot(h, w_ref[:, c * tn:(c + 1) * tn].astype(BF16))


def _in_proj(x2, g, w_all, layer, tm=1024, tn=512):
    m, d = x2.shape
    n = w_all.shape[2]
    return pl.pallas_call(
        functools.partial(_in_proj_kernel, tn=tn),
        grid=(m // tm,),
        in_specs=[
            pl.BlockSpec((tm, d), lambda i: (i, 0)),
            pl.BlockSpec((1, d), lambda i: (0, 0), pipeline_mode=pl.Buffered(1)),
            pl.BlockSpec((None, d, n), lambda i: (layer, 0, 0), pipeline_mode=pl.Buffered(1)),
        ],
        out_specs=pl.BlockSpec((tm, n), lambda i: (i, 0)),
        out_shape=jax.ShapeDtypeStruct((m, n), F32),
        compiler_params=_params("parallel"),
        name="in_proj",
    )(x2, g.reshape(1, d), w_all)


def _rwkv_kernel(ya_ref, mu_ref, wlo_ref, w0a0_ref, g2_ref, vecs_ref, tri_ref, hsum_ref, o_ref,
                 state_ref, prev_ref, at_ref, rt_ref, bh_ref, kh_ref, v_ref, cum_ref, y_ref, vr_ref, *, ts):
    nh, hd, c_len = RWKV_HEADS, HEAD_DIM, RWKV_CHUNK

    @pl.when(pl.program_id(1) == 0)
    def _():
        state_ref[...] = jnp.zeros_like(state_ref)
        prev_ref[...] = jnp.zeros_like(prev_ref)

    ya = ya_ref[0]
    row = lax.broadcasted_iota(jnp.int32, ya.shape, 0)
    prev = jnp.where(row == 0, prev_ref[0:1, :], pltpu.roll(ya, 1, axis=0))
    prev_ref[0:1, :] = ya[ts - 1:ts, :]
    xs = ya + mu_ref[...] * (prev - ya)

    r = xs[:, 0:D_RWKV]
    k = xs[:, D_RWKV:2 * D_RWKV]
    v = xs[:, 2 * D_RWKV:3 * D_RWKV]
    wa = xs[:, 3 * D_RWKV:3 * D_RWKV + LANE]
    gd = xs[:, 3 * D_RWKV + LANE:A_COLS]
    lane = lax.broadcasted_iota(jnp.int32, wa.shape, 1)
    wa = jnp.where(lane < DECAY_LORA, jnp.tanh(wa), wa)
    lo = _mm(wa, wlo_ref[...]) + w0a0_ref[...]
    z = -lo[:, 0:D_RWKV]
    softplus = jnp.maximum(z, 0.0) + jnp.log(1.0 + jnp.exp(-jnp.abs(z)))
    logdecay = -jnp.exp(-softplus - 0.5)
    alpha = _sigmoid(lo[:, D_RWKV:2 * D_RWKV])
    gate = _mm(_sigmoid(gd), g2_ref[...])

    k_k = vecs_ref[0:1, :]
    k_a = vecs_ref[1:2, :]
    r_k = vecs_ref[2:3, :]
    lnx_g = vecs_ref[3:4, :]
    lnx_b = vecs_ref[4:5, :]
    hsum = hsum_ref[...]

    kk = k * k_k
    kk = kk / jnp.maximum(jnp.sqrt(_mm(kk * kk, hsum)), 1e-12)
    k2 = k * (1.0 + (alpha - 1.0) * k_a)
    tri = tri_ref[...]
    tr = tri.shape[0]
    cum = jnp.concatenate([_dot_exact_lhs(tri, logdecay[i * tr:(i + 1) * tr]) for i in range(ts // tr)],
                          axis=0)
    e_neg = jnp.exp(-cum)
    at = -kk * jnp.exp(cum - logdecay)
    rt = r * jnp.exp(cum)
    bh = kk * alpha * e_neg
    kh = k2 * e_neg
    zeros_half = jnp.zeros((ts, hd), F32)
    for h in range(nh):
        sl = slice(h * hd, (h + 1) * hd)
        at_ref[h] = at[:, sl]
        rt_ref[h] = rt[:, sl]
        bh_ref[h] = bh[:, sl]
        kh_ref[h] = kh[:, sl]
        v_ref[h] = v[:, sl]
        vr_ref[h] = jnp.concatenate([zeros_half, v[:, sl]], axis=1)
        cum_ref[h] = cum[:, sl]

    ri = lax.broadcasted_iota(jnp.int32, (c_len, 2 * c_len), 0)
    ci = lax.broadcasted_iota(jnp.int32, (c_len, 2 * c_len), 1)
    ci = jnp.where(ci >= c_len, ci - c_len, ci)
    strict = ri > ci
    incl = ri >= ci
    eye = (lax.broadcasted_iota(jnp.int32, (c_len, c_len), 0)
           == lax.broadcasted_iota(jnp.int32, (c_len, c_len), 1))
    zeros_rows = jnp.zeros((c_len, hd), F32)
    n_sq = c_len.bit_length() - 1

    state = [state_ref[h] for h in range(nh)]

    def cut(ref, u):
        cc, h = u
        return ref[h, cc * c_len:(cc + 1) * c_len, :]

    for g0 in range(0, ts // c_len, RWKV_GROUP_CHUNKS):
        units = [(cc, h) for cc in range(g0, g0 + RWKV_GROUP_CHUNKS) for h in range(nh)]
        at_c = [cut(at_ref, u) for u in units]
        rt_c = [cut(rt_ref, u) for u in units]
        bh_c = [cut(bh_ref, u) for u in units]
        kh_c = [cut(kh_ref, u) for u in units]
        g_c = [jnp.exp(cum_ref[h, (cc + 1) * c_len - 1:(cc + 1) * c_len, :]) for cc, h in units]
        p_all = [_mm(jnp.concatenate([a, r_], axis=0), jnp.concatenate([b_, k_], axis=0), NT)
                 for a, r_, b_, k_ in zip(at_c, rt_c, bh_c, kh_c)]
        top = [jnp.where(strict, pa[0:c_len], 0.0) for pa in p_all]
        bot = [jnp.where(incl, pa[c_len:], 0.0) for pa in p_all]
        x2 = [_mm(t, jnp.concatenate([zeros_rows, cut(v_ref, u)], axis=0)) for t, u in zip(top, units)]
        x = [jnp.concatenate([a, b_], axis=1) for a, b_ in zip(at_c, x2)]
        p = [t[:, 0:c_len] for t in top]
        for it in range(n_sq):
            if it + 1 < n_sq:
                px = [_mm(pi, jnp.concatenate([xi, pi], axis=1)) for pi, xi in zip(p, x)]
                x = [xi + r_[:, 0:2 * hd] for xi, r_ in zip(x, px)]
                p = [r_[:, 2 * hd:] for r_ in px]
            else:
                x = [xi + _mm(pi, xi) for pi, xi in zip(p, x)]
        rhs = [jnp.concatenate([xi, cut(vr_ref, u)], axis=0) for xi, u in zip(x, units)]
        res_y = [_mm(b_, r_) for b_, r_ in zip(bot, rhs)]
        res_m = [_mm(jnp.concatenate([b_ * g, k_ * g], axis=0), r_, TN)
                 for b_, k_, g, r_ in zip(bh_c, kh_c, g_c, rhs)]
        r_eff = [r_ + ry[:, 0:hd] for r_, ry in zip(rt_c, res_y)]
        g_mat = [jnp.where(eye, g, 0.0) + rm[:, 0:hd] for g, rm in zip(g_c, res_m)]
        for i, (cc, h) in enumerate(units):
            prod = _mm(jnp.concatenate([r_eff[i], g_mat[i]], axis=0), state[h])
            y_ref[h, cc * c_len:(cc + 1) * c_len, :] = prod[0:c_len] + res_y[i][:, hd:]
            state[h] = prod[c_len:] + res_m[i][:, hd:]
    for h in range(nh):
        state_ref[h] = state[h]

    y = jnp.concatenate([y_ref[h] for h in range(nh)], axis=-1)
    inv_hd = 1.0 / hd
    mean = _mm(y, hsum) * inv_hd
    yc = y - mean
    var = _mm(yc * yc, hsum) * inv_hd
    yn = yc * lax.rsqrt(var + RWKV_LNX_EPS) * lnx_g + lnx_b
    bonus = _mm(r * k2 * r_k, hsum) * v
    o_ref[0] = ((yn + bonus) * gate).astype(o_ref.dtype)


def _rwkv(proj, mu, w0, w2, a0, a2, g2, k_k, k_a, r_k, lnx_g, lnx_b, ts=512):
    b, s, _ = proj.shape
    wlo = jnp.zeros((LANE, 2 * D_RWKV), F32)
    wlo = wlo.at[0:DECAY_LORA, 0:D_RWKV].set(w2).at[DECAY_LORA:LANE, D_RWKV:].set(a2)
    w0a0 = jnp.concatenate([w0, a0]).reshape(1, 2 * D_RWKV)
    vecs = jnp.zeros((SUBLANE, D_RWKV), F32)
    vecs = vecs.at[0].set(k_k).at[1].set(k_a).at[2].set(r_k.reshape(-1)).at[3].set(lnx_g).at[4].set(lnx_b)
    tri_rows = min(ts, MXU_DEPTH)
    ti = jnp.arange(tri_rows)
    tri = ((ti[:, None] >= ti[None, :]) & (ti[:, None] // RWKV_CHUNK == ti[None, :] // RWKV_CHUNK)).astype(BF16)
    hi_ = jnp.arange(D_RWKV) // HEAD_DIM
    hsum = (hi_[:, None] == hi_[None, :]).astype(BF16)
    const = lambda shape: pl.BlockSpec(shape, lambda i, j: (0,) * len(shape))
    head_buf = pltpu.VMEM((RWKV_HEADS, ts, HEAD_DIM), F32)
    return pl.pallas_call(
        functools.partial(_rwkv_kernel, ts=ts),
        grid=(b, s // ts),
        in_specs=[
            pl.BlockSpec((1, ts, A_COLS), lambda i, j: (i, j, 0)),
            const((1, A_COLS)),
            const((LANE, 2 * D_RWKV)),
            const((1, 2 * D_RWKV)),
            const((GATE_LORA, D_RWKV)),
            const((SUBLANE, D_RWKV)),
            const((tri_rows, tri_rows)),
            const((D_RWKV, D_RWKV)),
        ],
        out_specs=pl.BlockSpec((1, ts, D_RWKV), lambda i, j: (i, j, 0)),
        out_shape=jax.ShapeDtypeStruct((b, s, D_RWKV), BF16),
        scratch_shapes=[
            pltpu.VMEM((RWKV_HEADS, HEAD_DIM, HEAD_DIM), F32),
            pltpu.VMEM((SUBLANE, A_COLS), F32),
            head_buf, head_buf, head_buf, head_buf, head_buf, head_buf, head_buf,
            pltpu.VMEM((RWKV_HEADS, ts, 2 * HEAD_DIM), F32),
        ],
        compiler_params=_params("parallel", "arbitrary"),
        name="rwkv7",
    )(proj, mu.reshape(1, A_COLS), wlo, w0a0, g2, vecs, tri, hsum)


def _sublane_allreduce(x, op):
    shift = SUBLANE // 2
    while shift:
        x = op(x, pltpu.roll(x, shift, axis=0))
        shift //= 2
    return x


def _moba_kernel(q_ref, k_ref, v_ref, o_ref, *, s):
    blk = MOBA_BLOCK
    nb = s // blk
    sub = SUBLANE
    q = q_ref[0]
    k = k_ref[0]
    k16 = k.astype(BF16)
    vt16 = jnp.concatenate([v_ref[0].T, jnp.ones((sub, s), F32)], axis=0).astype(BF16)
    lane = lax.broadcasted_iota(jnp.int32, (1, LANE), 1)
    kbar = jnp.concatenate(
        [jnp.mean(k[n * blk:(n + 1) * blk], axis=0, keepdims=True) for n in range(nb)], axis=0)
    key_i = lax.broadcasted_iota(jnp.int32, (blk, blk), 0)
    qry_i = lax.broadcasted_iota(jnp.int32, (blk, blk), 1)
    neg_inf = -jnp.inf
    causal_pen = jnp.where(key_i <= qry_i, 0.0, neg_inf).reshape(blk // sub, sub, blk)
    blk_id = lax.broadcasted_iota(jnp.int32, (nb, blk), 0)
    out_row = lax.broadcasted_iota(jnp.int32, (LANE, blk), 0)
    n_top = min(MOBA_TOPK, nb - 1)

    q_groups = [[qi for qi in range(nb) if qi % 4 in (0, 3)], [qi for qi in range(nb) if qi % 4 in (1, 2)]]
    for group in q_groups:
        units = [(qi, hh) for qi in group for hh in range(2)]
        logits = []
        for qi, hh in units:
            rows = slice(qi * blk, (qi + 1) * blk)
            head_lanes = (lane >= hh * HEAD_DIM) & (lane < (hh + 1) * HEAD_DIM)
            qh = jnp.where(head_lanes, q[rows], 0.0)
            qh16 = (qh * (HEAD_DIM ** -0.5 * LOG2_E)).T.astype(BF16)
            first = 0 if n_top > 0 else qi
            kq = lambda n: _dot(k16[n * blk:(n + 1) * blk], qh16).reshape(blk // sub, sub, blk)
            lg3 = []
            if qi > first:
                sc = _dot(kbar, qh, NT, HI)
                cnt = jnp.zeros((nb, blk), F32)
                for m in range(qi):
                    cm = sc[m:m + 1, :]
                    beats = (cm > sc) | ((cm == sc) & (blk_id > m))
                    cnt = cnt + jnp.where(beats, 1.0, 0.0)
                pen = jnp.where((cnt < n_top) & (blk_id < qi), 0.0, neg_inf)
                for n in range(qi):
                    lg3.append(kq(n) + jnp.broadcast_to(pen[n:n + 1, :], (sub, blk))[None])
            lg3.append(kq(qi) + causal_pen)
            logits.append((first, lg3))
        m_rep = [_sublane_allreduce(functools.reduce(jnp.maximum, [jnp.max(lg, axis=0) for lg in lgs]),
                                    jnp.maximum) for _, lgs in logits]
        outs = []
        for (first, lgs), m_u in zip(logits, m_rep):
            acc = jnp.zeros((LANE + sub, blk), F32)
            for i, lg in enumerate(lgs):
                p = jnp.exp2(lg - m_u[None])
                kv = first + i
                acc = acc + _dot(vt16[:, kv * blk:(kv + 1) * blk], p.reshape(blk, blk).astype(BF16))
            l_rep = acc[LANE:]
            outs.append((acc[0:LANE].reshape(LANE // sub, sub, blk) / l_rep[None]).reshape(LANE, blk))
        for j, qi in enumerate(group):
            out_t = jnp.where(out_row < HEAD_DIM, outs[2 * j], outs[2 * j + 1])
            o_ref[0, qi * blk:(qi + 1) * blk, :] = out_t.T.astype(o_ref.dtype)


def _moba(proj):
    b, s, _ = proj.shape
    n_pair = D_MOBA // LANE
    base = A_COLS // LANE
    spec = lambda off: pl.BlockSpec((1, s, LANE), lambda i, p: (i, 0, base + off + p))
    return pl.pallas_call(
        functools.partial(_moba_kernel, s=s),
        grid=(b, n_pair),
        in_specs=[spec(0), spec(n_pair), spec(2 * n_pair)],
        out_specs=pl.BlockSpec((1, s, LANE), lambda i, p: (i, 0, p)),
        out_shape=jax.ShapeDtypeStruct((b, s, D_MOBA), BF16),
        compiler_params=_params("parallel", "parallel"),
        name="moba",
    )(proj, proj, proj)


def _gelu(x):
    return 0.5 * x * (1.0 + jnp.tanh(0.7978845608028654 * (x + 0.044715 * (x * x * x))))


def _gmlp_kernel(u_ref, v_ref, lng_ref, lnb_ref, ws_ref, bias_ref, o_ref, *, ts):
    ch = GMLP_CHUNK
    u = _gelu(u_ref[0])
    v = _gelu(v_ref[0])
    mean = jnp.mean(v, axis=-1, keepdims=True)
    vc = v - mean
    var = jnp.mean(vc * vc, axis=-1, keepdims=True)
    vn = ((vc * lax.rsqrt(var + NORM_EPS)) * lng_ref[...] + lnb_ref[...]).astype(BF16)
    ri = lax.broadcasted_iota(jnp.int32, (ch, ch), 0)
    ci = lax.broadcasted_iota(jnp.int32, (ch, ch), 1)
    tril = ri >= ci
    w = [jnp.where(tril, ws_ref[g], 0.0).astype(BF16) for g in range(GMLP_GROUPS)]
    lane = lax.broadcasted_iota(jnp.int32, (1, D_GMLP), 1)
    bias = bias_ref[...]
    for c in range(ts // ch):
        rows = slice(c * ch, (c + 1) * ch)
        vchunk = vn[rows]
        mixed = _dot(w[0], vchunk)
        for g in range(1, GMLP_GROUPS):
            mixed = jnp.where(lane >= g * HEAD_DIM, _dot(w[g], vchunk), mixed)
        o_ref[0, rows, :] = (u[rows] * (mixed + bias)).astype(o_ref.dtype)


def _gmlp(proj, ln_g, ln_b, w_s, b_s, ts=1024):
    b, s, _ = proj.shape
    base = (A_COLS + B_COLS) // D_GMLP
    bias = jnp.repeat(b_s.T, HEAD_DIM, axis=1)
    const = lambda shape: pl.BlockSpec(shape, lambda i, j: (0,) * len(shape))
    return pl.pallas_call(
        functools.partial(_gmlp_kernel, ts=ts),
        grid=(b, s // ts),
        in_specs=[
            pl.BlockSpec((1, ts, D_GMLP), lambda i, j: (i, j, base)),
            pl.BlockSpec((1, ts, D_GMLP), lambda i, j: (i, j, base + 1)),
            const((1, D_GMLP)),
            const((1, D_GMLP)),
            const((GMLP_GROUPS, GMLP_CHUNK, GMLP_CHUNK)),
            const((GMLP_CHUNK, D_GMLP)),
        ],
        out_specs=pl.BlockSpec((1, ts, D_GMLP), lambda i, j: (i, j, 0)),
        out_shape=jax.ShapeDtypeStruct((b, s, D_GMLP), BF16),
        compiler_params=_params("parallel", "parallel"),
        name="gmlp",
    )(proj, proj, ln_g.reshape(1, D_GMLP), ln_b.reshape(1, D_GMLP), w_s, bias)


def _out_proj_kernel(a_ref, b_ref, c_ref, w_ref, x_ref, g_ref, o_ref):
    mixed_in = jnp.concatenate([a_ref[...], b_ref[...], c_ref[...]], axis=-1)
    mix = _dot(mixed_in, w_ref[...].astype(BF16))
    o_ref[...] = x_ref[...] + _rms(mix, g_ref[...])


def _out_proj(a2, b2, c2, w_all, layer, x2, g, tm=1024):
    m, d = x2.shape
    row = lambda n: pl.BlockSpec((tm, n), lambda i: (i, 0))
    return pl.pallas_call(
        _out_proj_kernel,
        grid=(m // tm,),
        in_specs=[row(D_RWKV), row(D_MOBA), row(D_GMLP),
                  pl.BlockSpec((None, D_MODEL, d), lambda i: (layer, 0, 0), pipeline_mode=pl.Buffered(1)),
                  row(d), pl.BlockSpec((1, d), lambda i: (0, 0))],
        out_specs=row(d),
        out_shape=jax.ShapeDtypeStruct((m, d), F32),
        compiler_params=_params("parallel"),
        name="out_proj",
    )(a2, b2, c2, w_all, x2, g.reshape(1, d))


def _ffn_kernel(x_ref, gpre_ref, wi_ref, wo_ref, gpost_ref, o_ref, *, tf):
    x = x_ref[...]
    h = _rms(x, gpre_ref[...]).astype(BF16)
    f = None
    for c in range(D_FF // tf):
        gate = _dot(h, wi_ref[:, c * tf:(c + 1) * tf].astype(BF16))
        up = _dot(h, wi_ref[:, D_FF + c * tf:D_FF + (c + 1) * tf].astype(BF16))
        act = (gate * _sigmoid(gate) * up).astype(BF16)
        part = _dot(act, wo_ref[c * tf:(c + 1) * tf, :].astype(BF16))
        f = part if f is None else f + part
    o_ref[...] = x + _rms(f, gpost_ref[...])


def _ffn(x2, g_pre, w_in_all, w_out_all, layer, g_post, tm=512, tf=256):
    m, d = x2.shape
    resident = lambda shape: pl.BlockSpec(shape, lambda i: (0, 0), pipeline_mode=pl.Buffered(1))
    slab = lambda shape: pl.BlockSpec((None,) + shape, lambda i: (layer, 0, 0), pipeline_mode=pl.Buffered(1))
    return pl.pallas_call(
        functools.partial(_ffn_kernel, tf=tf),
        grid=(m // tm,),
        in_specs=[
            pl.BlockSpec((tm, d), lambda i: (i, 0)),
            resident((1, d)),
            slab((d, 2 * D_FF)),
            slab((D_FF, d)),
            resident((1, d)),
        ],
        out_specs=pl.BlockSpec((tm, d), lambda i: (i, 0)),
        out_shape=jax.ShapeDtypeStruct((m, d), F32),
        compiler_params=_params("parallel"),
        name="ffn",
    )(x2, g_pre.reshape(1, d), w_in_all, w_out_all, g_post.reshape(1, d))


def kernel(x, pre_mix_g, w_in, rwkv_mu, rwkv_w0, rwkv_w2, rwkv_a0, rwkv_a2, rwkv_g2, rwkv_k_k, rwkv_k_a, rwkv_r_k, rwkv_lnx_g, rwkv_lnx_b, gmlp_ln_g, gmlp_ln_b, gmlp_w_s, gmlp_b_s, w_out, post_mix_g, pre_ffn_g, w_ffn_in, w_ffn_out, post_ffn_g):
    b, s, d = x.shape
    m = b * s
    x2 = x.reshape(m, d)
    for l in range(w_in.shape[0]):
        proj = _in_proj(x2, pre_mix_g[l], w_in, l).reshape(b, s, D_IN)
        a_out = _rwkv(proj, rwkv_mu[l], rwkv_w0[l], rwkv_w2[l], rwkv_a0[l], rwkv_a2[l], rwkv_g2[l],
                      rwkv_k_k[l], rwkv_k_a[l], rwkv_r_k[l], rwkv_lnx_g[l], rwkv_lnx_b[l])
        b_out = _moba(proj)
        c_out = _gmlp(proj, gmlp_ln_g[l], gmlp_ln_b[l], gmlp_w_s[l], gmlp_b_s[l])
        x2 = _out_proj(a_out.reshape(m, D_RWKV), b_out.reshape(m, D_MOBA), c_out.reshape(m, D_GMLP),
                       w_out, l, x2, post_mix_g[l])
        x2 = _ffn(x2, pre_ffn_g[l], w_ffn_in, w_ffn_out, l, post_ffn_g[l])
    return x2.reshape(b, s, d)
```

```python
import functools

import jax
import jax.numpy as jnp
from jax import lax
from jax.experimental import pallas as pl
from jax.experimental.pallas import tpu as pltpu

F32 = jnp.float32
BF16 = jnp.bfloat16

D_MODEL = 1024
HEAD_DIM = 64
RWKV_HEADS = 6
D_RWKV = RWKV_HEADS * HEAD_DIM
DECAY_LORA = 64
AAA_LORA = 64
GATE_LORA = 128
RWKV_LNX_EPS = 64e-5
MOBA_HEADS = 6
D_MOBA = MOBA_HEADS * HEAD_DIM
MOBA_BLOCK = 256
MOBA_TOPK = 3
GMLP_GROUPS = 4
D_GMLP = GMLP_GROUPS * HEAD_DIM
GMLP_CHUNK = 128
D_FF = 2816
NORM_EPS = 1e-6
LOG2_E = 1.4426950408889634
A_COLS = 3 * D_RWKV + DECAY_LORA + AAA_LORA + GATE_LORA
B_COLS = 3 * D_MOBA
C_COLS = 2 * D_GMLP
D_IN = A_COLS + B_COLS + C_COLS

LANE = 128
SUBLANE = 8
MXU_DEPTH = 256
RWKV_CHUNK = 64
RWKV_GROUP_CHUNKS = 4
VMEM_LIMIT = 56 * 1024 * 1024

NN = (((1,), (0,)), ((), ()))
NT = (((1,), (1,)), ((), ()))
TN = (((0,), (0,)), ((), ()))
HI = lax.Precision.HIGHEST


def _dot(a, b, dims=NN, precision=None):
    return lax.dot_general(a, b, dims, precision=precision, preferred_element_type=F32)


def _mm(a, b, dims=NN):
    return _dot(a.astype(BF16), b.astype(BF16), dims)


def _split2(x):
    hi = x.astype(BF16)
    lo = (x - hi.astype(F32)).astype(BF16)
    return hi, lo


def _dot_exact_lhs(w_bf16, x):
    hi, lo = _split2(x)
    return _dot(w_bf16, hi) + _dot(w_bf16, lo)


def _sigmoid(x):
    return 1.0 / (1.0 + jnp.exp(-x))


def _rms(x, g):
    return x * lax.rsqrt(jnp.mean(x * x, axis=-1, keepdims=True) + NORM_EPS) * g


def _params(*sem):
    return pltpu.CompilerParams(dimension_semantics=sem, vmem_limit_bytes=VMEM_LIMIT)


def _in_proj_kernel(x_ref, g_ref, w_ref, o_ref, *, tn):
    h = _rms(x_ref[...], g_ref[...]).astype(BF16)
    for c in range(w_ref.shape[1] // tn):
        o_ref[:, c * tn:(c + 1) * tn] = _dot(h, w_ref[:, c * tn:(c + 1) * tn].astype(BF16))


def _in_proj(x2, g, w_all, layer, tm=1024, tn=512):
    m, d = x2.shape
    n = w_all.shape[2]
    return pl.pallas_call(
        functools.partial(_in_proj_kernel, tn=tn),
        grid=(m // tm,),
        in_specs=[
            pl.BlockSpec((tm, d), lambda i: (i, 0)),
            pl.BlockSpec((1, d), lambda i: (0, 0), pipeline_mode=pl.Buffered(1)),
            pl.BlockSpec((None, d, n), lambda i: (layer, 0, 0), pipeline_mode=pl.Buffered(1)),
        ],
        out_specs=pl.BlockSpec((tm, n), lambda i: (i, 0)),
        out_shape=jax.ShapeDtypeStruct((m, n), F32),
        compiler_params=_params("parallel"),
        name="in_proj",
    )(x2, g.reshape(1, d), w_all)


def _rwkv_kernel(ya_ref, mu_ref, wlo_ref, w0a0_ref, g2_ref, vecs_ref, tri_ref, hsum_ref, o_ref,
                 state_ref, prev_ref, at_ref, rt_ref, bh_ref, kh_ref, v_ref, cum_ref, y_ref, vr_ref, *, ts):
    nh, hd, c_len = RWKV_HEADS, HEAD_DIM, RWKV_CHUNK

    @pl.when(pl.program_id(1) == 0)
    def _():
        state_ref[...] = jnp.zeros_like(state_ref)
        prev_ref[...] = jnp.zeros_like(prev_ref)

    ya = ya_ref[0]
    row = lax.broadcasted_iota(jnp.int32, ya.shape, 0)
    prev = jnp.where(row == 0, prev_ref[0:1, :], pltpu.roll(ya, 1, axis=0))
    prev_ref[0:1, :] = ya[ts - 1:ts, :]
    xs = ya + mu_ref[...] * (prev - ya)

    r = xs[:, 0:D_RWKV]
    k = xs[:, D_RWKV:2 * D_RWKV]
    v = xs[:, 2 * D_RWKV:3 * D_RWKV]
    wa = xs[:, 3 * D_RWKV:3 * D_RWKV + LANE]
    gd = xs[:, 3 * D_RWKV + LANE:A_COLS]
    lane = lax.broadcasted_iota(jnp.int32, wa.shape, 1)
    wa = jnp.where(lane < DECAY_LORA, jnp.tanh(wa), wa)
    lo = _mm(wa, wlo_ref[...]) + w0a0_ref[...]
    z = -lo[:, 0:D_RWKV]
    softplus = jnp.maximum(z, 0.0) + jnp.log(1.0 + jnp.exp(-jnp.abs(z)))
    logdecay = -jnp.exp(-softplus - 0.5)
    alpha = _sigmoid(lo[:, D_RWKV:2 * D_RWKV])
    gate = _mm(_sigmoid(gd), g2_ref[...])

    k_k = vecs_ref[0:1, :]
    k_a = vecs_ref[1:2, :]
    r_k = vecs_ref[2:3, :]
    lnx_g = vecs_ref[3:4, :]
    lnx_b = vecs_ref[4:5, :]
    hsum = hsum_ref[...]

    kk = k * k_k
    kk = kk / jnp.maximum(jnp.sqrt(_mm(kk * kk, hsum)), 1e-12)
    k2 = k * (1.0 + (alpha - 1.0) * k_a)
    tri = tri_ref[...]
    tr = tri.shape[0]
    cum = jnp.concatenate([_dot_exact_lhs(tri, logdecay[i * tr:(i + 1) * tr]) for i in range(ts // tr)],
                          axis=0)
    e_neg = jnp.exp(-cum)
    at = -kk * jnp.exp(cum - logdecay)
    rt = r * jnp.exp(cum)
    bh = kk * alpha * e_neg
    kh = k2 * e_neg
    zeros_half = jnp.zeros((ts, hd), F32)
    for h in range(nh):
        sl = slice(h * hd, (h + 1) * hd)
        at_ref[h] = at[:, sl]
        rt_ref[h] = rt[:, sl]
        bh_ref[h] = bh[:, sl]
        kh_ref[h] = kh[:, sl]
        v_ref[h] = v[:, sl]
        vr_ref[h] = jnp.concatenate([zeros_half, v[:, sl]], axis=1)
        cum_ref[h] = cum[:, sl]

    ri = lax.broadcasted_iota(jnp.int32, (c_len, 2 * c_len), 0)
    ci = lax.broadcasted_iota(jnp.int32, (c_len, 2 * c_len), 1)
    ci = jnp.where(ci >= c_len, ci - c_len, ci)
    strict = ri > ci
    incl = ri >= ci
    eye = (lax.broadcasted_iota(jnp.int32, (c_len, c_len), 0)
           == lax.broadcasted_iota(jnp.int32, (c_len, c_len), 1))
    zeros_rows = jnp.zeros((c_len, hd), F32)
    n_sq = c_len.bit_length() - 1

    state = [state_ref[h] for h in range(nh)]

    def cut(ref, u):
        cc, h = u
        return ref[h, cc * c_len:(cc + 1) * c_len, :]

    for g0 in range(0, ts // c_len, RWKV_GROUP_CHUNKS):
        units = [(cc, h) for cc in range(g0, g0 + RWKV_GROUP_CHUNKS) for h in range(nh)]
        at_c = [cut(at_ref, u) for u in units]
        rt_c = [cut(rt_ref, u) for u in units]
        bh_c = [cut(bh_ref, u) for u in units]
        kh_c = [cut(kh_ref, u) for u in units]
        g_c = [jnp.exp(cum_ref[h, (cc + 1) * c_len - 1:(cc + 1) * c_len, :]) for cc, h in units]
        p_all = [_mm(jnp.concatenate([a, r_], axis=0), jnp.concatenate([b_, k_], axis=0), NT)
                 for a, r_, b_, k_ in zip(at_c, rt_c, bh_c, kh_c)]
        top = [jnp.where(strict, pa[0:c_len], 0.0) for pa in p_all]
        bot = [jnp.where(incl, pa[c_len:], 0.0) for pa in p_all]
        x2 = [_mm(t, jnp.concatenate([zeros_rows, cut(v_ref, u)], axis=0)) for t, u in zip(top, units)]
        x = [jnp.concatenate([a, b_], axis=1) for a, b_ in zip(at_c, x2)]
        p = [t[:, 0:c_len] for t in top]
        for it in range(n_sq):
            if it + 1 < n_sq:
                px = [_mm(pi, jnp.concatenate([xi, pi], axis=1)) for pi, xi in zip(p, x)]
                x = [xi + r_[:, 0:2 * hd] for xi, r_ in zip(x, px)]
                p = [r_[:, 2 * hd:] for r_ in px]
            else:
                x = [xi + _mm(pi, xi) for pi, xi in zip(p, x)]
        rhs = [jnp.concatenate([xi, cut(vr_ref, u)], axis=0) for xi, u in zip(x, units)]
        res_y = [_mm(b_, r_) for b_, r_ in zip(bot, rhs)]
        res_m = [_mm(jnp.concatenate([b_ * g, k_ * g], axis=0), r_, TN)
                 for b_, k_, g, r_ in zip(bh_c, kh_c, g_c, rhs)]
        r_eff = [r_ + ry[:, 0:hd] for r_, ry in zip(rt_c, res_y)]
        g_mat = [jnp.where(eye, g, 0.0) + rm[:, 0:hd] for g, rm in zip(g_c, res_m)]
        for i, (cc, h) in enumerate(units):
            prod = _mm(jnp.concatenate([r_eff[i], g_mat[i]], axis=0), state[h])
            y_ref[h, cc * c_len:(cc + 1) * c_len, :] = prod[0:c_len] + res_y[i][:, hd:]
            state[h] = prod[c_len:] + res_m[i][:, hd:]
    for h in range(nh):
        state_ref[h] = state[h]

    y = jnp.concatenate([y_ref[h] for h in range(nh)], axis=-1)
    inv_hd = 1.0 / hd
    mean = _mm(y, hsum) * inv_hd
    yc = y - mean
    var = _mm(yc * yc, hsum) * inv_hd
    yn = yc * lax.rsqrt(var + RWKV_LNX_EPS) * lnx_g + lnx_b
    bonus = _mm(r * k2 * r_k, hsum) * v
    o_ref[0] = ((yn + bonus) * gate).astype(o_ref.dtype)


def _rwkv(proj, mu, w0, w2, a0, a2, g2, k_k, k_a, r_k, lnx_g, lnx_b, ts=512):
    b, s, _ = proj.shape
    wlo = jnp.zeros((LANE, 2 * D_RWKV), F32)
    wlo = wlo.at[0:DECAY_LORA, 0:D_RWKV].set(w2).at[DECAY_LORA:LANE, D_RWKV:].set(a2)
    w0a0 = jnp.concatenate([w0, a0]).reshape(1, 2 * D_RWKV)
    vecs = jnp.zeros((SUBLANE, D_RWKV), F32)
    vecs = vecs.at[0].set(k_k).at[1].set(k_a).at[2].set(r_k.reshape(-1)).at[3].set(lnx_g).at[4].set(lnx_b)
    tri_rows = min(ts, MXU_DEPTH)
    ti = jnp.arange(tri_rows)
    tri = ((ti[:, None] >= ti[None, :]) & (ti[:, None] // RWKV_CHUNK == ti[None, :] // RWKV_CHUNK)).astype(BF16)
    hi_ = jnp.arange(D_RWKV) // HEAD_DIM
    hsum = (hi_[:, None] == hi_[None, :]).astype(BF16)
    const = lambda shape: pl.BlockSpec(shape, lambda i, j: (0,) * len(shape))
    head_buf = pltpu.VMEM((RWKV_HEADS, ts, HEAD_DIM), F32)
    return pl.pallas_call(
        functools.partial(_rwkv_kernel, ts=ts),
        grid=(b, s // ts),
        in_specs=[
            pl.BlockSpec((1, ts, A_COLS), lambda i, j: (i, j, 0)),
            const((1, A_COLS)),
            const((LANE, 2 * D_RWKV)),
            const((1, 2 * D_RWKV)),
            const((GATE_LORA, D_RWKV)),
            const((SUBLANE, D_RWKV)),
            const((tri_rows, tri_rows)),
            const((D_RWKV, D_RWKV)),
        ],
        out_specs=pl.BlockSpec((1, ts, D_RWKV), lambda i, j: (i, j, 0)),
        out_shape=jax.ShapeDtypeStruct((b, s, D_RWKV), BF16),
        scratch_shapes=[
            pltpu.VMEM((RWKV_HEADS, HEAD_DIM, HEAD_DIM), F32),
            pltpu.VMEM((SUBLANE, A_COLS), F32),
            head_buf, head_buf, head_buf, head_buf, head_buf, head_buf, head_buf,
            pltpu.VMEM((RWKV_HEADS, ts, 2 * HEAD_DIM), F32),
        ],
        compiler_params=_params("parallel", "arbitrary"),
        name="rwkv7",
    )(proj, mu.reshape(1, A_COLS), wlo, w0a0, g2, vecs, tri, hsum)


def _sublane_allreduce(x, op):
    shift = SUBLANE // 2
    while shift:
        x = op(x, pltpu.roll(x, shift, axis=0))
        shift //= 2
    return x


def _moba_kernel(q_ref, k_ref, v_ref, o_ref, *, s):
    blk = MOBA_BLOCK
    nb = s // blk
    sub = SUBLANE
    q = q_ref[0]
    k = k_ref[0]
    k16 = k.astype(BF16)
    vt16 = jnp.concatenate([v_ref[0].T, jnp.ones((sub, s), F32)], axis=0).astype(BF16)
    lane = lax.broadcasted_iota(jnp.int32, (1, LANE), 1)
    kbar = jnp.concatenate(
        [jnp.mean(k[n * blk:(n + 1) * blk], axis=0, keepdims=True) for n in range(nb)], axis=0)
    key_i = lax.broadcasted_iota(jnp.int32, (blk, blk), 0)
    qry_i = lax.broadcasted_iota(jnp.int32, (blk, blk), 1)
    neg_inf = -jnp.inf
    causal_pen = jnp.where(key_i <= qry_i, 0.0, neg_inf).reshape(blk // sub, sub, blk)
    blk_id = lax.broadcasted_iota(jnp.int32, (nb, blk), 0)
    out_row = lax.broadcasted_iota(jnp.int32, (LANE, blk), 0)
    n_top = min(MOBA_TOPK, nb - 1)

    q_groups = [[qi for qi in range(nb) if qi % 4 in (0, 3)], [qi for qi in range(nb) if qi % 4 in (1, 2)]]
    for group in q_groups:
        units = [(qi, hh) for qi in group for hh in range(2)]
        logits = []
        for qi, hh in units:
            rows = slice(qi * blk, (qi + 1) * blk)
            head_lanes = (lane >= hh * HEAD_DIM) & (lane < (hh + 1) * HEAD_DIM)
            qh = jnp.where(head_lanes, q[rows], 0.0).T
            qh16 = (qh * (HEAD_DIM ** -0.5 * LOG2_E)).astype(BF16)
            first = 0 if n_top > 0 else qi
            kq = lambda n: _dot(k16[n * blk:(n + 1) * blk], qh16).reshape(blk // sub, sub, blk)
            lg3 = []
            if qi > first:
                sc = _dot(kbar, qh, NN, HI)
                cnt = jnp.zeros((nb, blk), F32)
                for m in range(qi):
                    cm = sc[m:m + 1, :]
                    beats = (cm > sc) | ((cm == sc) & (blk_id > m))
                    cnt = cnt + jnp.where(beats, 1.0, 0.0)
                pen = jnp.where((cnt < n_top) & (blk_id < qi), 0.0, neg_inf)
                for n in range(qi):
                    lg3.append(kq(n) + jnp.broadcast_to(pen[n:n + 1, :], (sub, blk))[None])
            lg3.append(kq(qi) + causal_pen)
            logits.append((first, lg3))
        m_rep = [_sublane_allreduce(functools.reduce(jnp.maximum, [jnp.max(lg, axis=0) for lg in lgs]),
                                    jnp.maximum) for _, lgs in logits]
        outs = []
        for (first, lgs), m_u in zip(logits, m_rep):
            acc = jnp.zeros((LANE + sub, blk), F32)
            for i, lg in enumerate(lgs):
                p = jnp.exp2(lg - m_u[None])
                kv = first + i
                acc = acc + _dot(vt16[:, kv * blk:(kv + 1) * blk], p.reshape(blk, blk).astype(BF16))
            l_rep = acc[LANE:]
            outs.append((acc[0:LANE].reshape(LANE // sub, sub, blk) / l_rep[None]).reshape(LANE, blk))
        for j, qi in enumerate(group):
            out_t = jnp.where(out_row < HEAD_DIM, outs[2 * j], outs[2 * j + 1])
            o_ref[0, qi * blk:(qi + 1) * blk, :] = out_t.T.astype(o_ref.dtype)


def _moba(proj):
    b, s, _ = proj.shape
    n_pair = D_MOBA // LANE
    base = A_COLS // LANE
    spec = lambda off: pl.BlockSpec((1, s, LANE), lambda i, p: (i, 0, base + off + p))
    return pl.pallas_call(
        functools.partial(_moba_kernel, s=s),
        grid=(b, n_pair),
        in_specs=[spec(0), spec(n_pair), spec(2 * n_pair)],
        out_specs=pl.BlockSpec((1, s, LANE), lambda i, p: (i, 0, p)),
        out_shape=jax.ShapeDtypeStruct((b, s, D_MOBA), BF16),
        compiler_params=_params("parallel", "parallel"),
        name="moba",
    )(proj, proj, proj)


def _gelu(x):
    return 0.5 * x * (1.0 + jnp.tanh(0.7978845608028654 * (x + 0.044715 * (x * x * x))))


def _gmlp_kernel(u_ref, v_ref, lng_ref, lnb_ref, ws_ref, bias_ref, o_ref, *, ts):
    ch = GMLP_CHUNK
    u = _gelu(u_ref[0])
    v = _gelu(v_ref[0])
    mean = jnp.mean(v, axis=-1, keepdims=True)
    vc = v - mean
    var = jnp.mean(vc * vc, axis=-1, keepdims=True)
    vn = ((vc * lax.rsqrt(var + NORM_EPS)) * lng_ref[...] + lnb_ref[...]).astype(BF16)
    ri = lax.broadcasted_iota(jnp.int32, (ch, ch), 0)
    ci = lax.broadcasted_iota(jnp.int32, (ch, ch), 1)
    tril = ri >= ci
    w = [jnp.where(tril, ws_ref[g], 0.0).astype(BF16) for g in range(GMLP_GROUPS)]
    lane = lax.broadcasted_iota(jnp.int32, (1, D_GMLP), 1)
    bias = bias_ref[...]
    for c in range(ts // ch):
        rows = slice(c * ch, (c + 1) * ch)
        vchunk = vn[rows]
        mixed = _dot(w[0], vchunk)
        for g in range(1, GMLP_GROUPS):
            mixed = jnp.where(lane >= g * HEAD_DIM, _dot(w[g], vchunk), mixed)
        o_ref[0, rows, :] = (u[rows] * (mixed + bias)).astype(o_ref.dtype)


def _gmlp(proj, ln_g, ln_b, w_s, b_s, ts=1024):
    b, s, _ = proj.shape
    base = (A_COLS + B_COLS) // D_GMLP
    bias = jnp.repeat(b_s.T, HEAD_DIM, axis=1)
    const = lambda shape: pl.BlockSpec(shape, lambda i, j: (0,) * len(shape))
    return pl.pallas_call(
        functools.partial(_gmlp_kernel, ts=ts),
        grid=(b, s // ts),
        in_specs=[
            pl.BlockSpec((1, ts, D_GMLP), lambda i, j: (i, j, base)),
            pl.BlockSpec((1, ts, D_GMLP), lambda i, j: (i, j, base + 1)),
            const((1, D_GMLP)),
            const((1, D_GMLP)),
            const((GMLP_GROUPS, GMLP_CHUNK, GMLP_CHUNK)),
            const((GMLP_CHUNK, D_GMLP)),
        ],
        out_specs=pl.BlockSpec((1, ts, D_GMLP), lambda i, j: (i, j, 0)),
        out_shape=jax.ShapeDtypeStruct((b, s, D_GMLP), BF16),
        compiler_params=_params("parallel", "parallel"),
        name="gmlp",
    )(proj, proj, ln_g.reshape(1, D_GMLP), ln_b.reshape(1, D_GMLP), w_s, bias)


def _out_proj_kernel(a_ref, b_ref, c_ref, w_ref, x_ref, g_ref, o_ref):
    mixed_in = jnp.concatenate([a_ref[...], b_ref[...], c_ref[...]], axis=-1)
    mix = _dot(mixed_in, w_ref[...].astype(BF16))
    o_ref[...] = x_ref[...] + _rms(mix, g_ref[...])


def _out_proj(a2, b2, c2, w_all, layer, x2, g, tm=1024):
    m, d = x2.shape
    row = lambda n: pl.BlockSpec((tm, n), lambda i: (i, 0))
    return pl.pallas_call(
        _out_proj_kernel,
        grid=(m // tm,),
        in_specs=[row(D_RWKV), row(D_MOBA), row(D_GMLP),
                  pl.BlockSpec((None, D_MODEL, d), lambda i: (layer, 0, 0), pipeline_mode=pl.Buffered(1)),
                  row(d), pl.BlockSpec((1, d), lambda i: (0, 0))],
        out_specs=row(d),
        out_shape=jax.ShapeDtypeStruct((m, d), F32),
        compiler_params=_params("parallel"),
        name="out_proj",
    )(a2, b2, c2, w_all, x2, g.reshape(1, d))


def _ffn_kernel(x_ref, gpre_ref, wi_ref, wo_ref, gpost_ref, o_ref, *, tf):
    x = x_ref[...]
    h = _rms(x, gpre_ref[...]).astype(BF16)
    f = None
    for c in range(D_FF // tf):
        gate = _dot(h, wi_ref[:, c * tf:(c + 1) * tf].astype(BF16))
        up = _dot(h, wi_ref[:, D_FF + c * tf:D_FF + (c + 1) * tf].astype(BF16))
        act = (gate * _sigmoid(gate) * up).astype(BF16)
        part = _dot(act, wo_ref[c * tf:(c + 1) * tf, :].astype(BF16))
        f = part if f is None else f + part
    o_ref[...] = x + _rms(f, gpost_ref[...])


def _ffn(x2, g_pre, w_in_all, w_out_all, layer, g_post, tm=512, tf=256):
    m, d = x2.shape
    resident = lambda shape: pl.BlockSpec(shape, lambda i: (0, 0), pipeline_mode=pl.Buffered(1))
    slab = lambda shape: pl.BlockSpec((None,) + shape, lambda i: (layer, 0, 0), pipeline_mode=pl.Buffered(1))
    return pl.pallas_call(
        functools.partial(_ffn_kernel, tf=tf),
        grid=(m // tm,),
        in_specs=[
            pl.BlockSpec((tm, d), lambda i: (i, 0)),
            resident((1, d)),
            slab((d, 2 * D_FF)),
            slab((D_FF, d)),
            resident((1, d)),
        ],
        out_specs=pl.BlockSpec((tm, d), lambda i: (i, 0)),
        out_shape=jax.ShapeDtypeStruct((m, d), F32),
        compiler_params=_params("parallel"),
        name="ffn",
    )(x2, g_pre.reshape(1, d), w_in_all, w_out_all, g_post.reshape(1, d))


def kernel(x, pre_mix_g, w_in, rwkv_mu, rwkv_w0, rwkv_w2, rwkv_a0, rwkv_a2, rwkv_g2, rwkv_k_k, rwkv_k_a, rwkv_r_k, rwkv_lnx_g, rwkv_lnx_b, gmlp_ln_g, gmlp_ln_b, gmlp_w_s, gmlp_b_s, w_out, post_mix_g, pre_ffn_g, w_ffn_in, w_ffn_out, post_ffn_g):
    b, s, d = x.shape
    m = b * s
    x2 = x.reshape(m, d)
    for l in range(w_in.shape[0]):
        proj = _in_proj(x2, pre_mix_g[l], w_in, l).reshape(b, s, D_IN)
        a_out = _rwkv(proj, rwkv_mu[l], rwkv_w0[l], rwkv_w2[l], rwkv_a0[l], rwkv_a2[l], rwkv_g2[l],
                      rwkv_k_k[l], rwkv_k_a[l], rwkv_r_k[l], rwkv_lnx_g[l], rwkv_lnx_b[l])
        b_out = _moba(proj)
        c_out = _gmlp(proj, gmlp_ln_g[l], gmlp_ln_b[l], gmlp_w_s[l], gmlp_b_s[l])
        x2 = _out_proj(a_out.reshape(m, D_RWKV), b_out.reshape(m, D_MOBA), c_out.reshape(m, D_GMLP),
                       w_out, l, x2, post_mix_g[l])
        x2 = _ffn(x2, pre_ffn_g[l], w_ffn_in, w_ffn_out, l, post_ffn_g[l])
    return x2.reshape(b, s, d)
```

```python
import functools

import jax
import jax.numpy as jnp
from jax import lax
from jax.experimental import pallas as pl
from jax.experimental.pallas import tpu as pltpu

F32 = jnp.float32
BF16 = jnp.bfloat16

D_MODEL = 1024
HEAD_DIM = 64
RWKV_HEADS = 6
D_RWKV = RWKV_HEADS * HEAD_DIM
DECAY_LORA = 64
AAA_LORA = 64
GATE_LORA = 128
RWKV_LNX_EPS = 64e-5
MOBA_HEADS = 6
D_MOBA = MOBA_HEADS * HEAD_DIM
MOBA_BLOCK = 256
MOBA_TOPK = 3
GMLP_GROUPS = 4
D_GMLP = GMLP_GROUPS * HEAD_DIM
GMLP_CHUNK = 128
D_FF = 2816
NORM_EPS = 1e-6
LOG2_E = 1.4426950408889634
A_COLS = 3 * D_RWKV + DECAY_LORA + AAA_LORA + GATE_LORA
B_COLS = 3 * D_MOBA
C_COLS = 2 * D_GMLP
D_IN = A_COLS + B_COLS + C_COLS

LANE = 128
SUBLANE = 8
MXU_DEPTH = 256
RWKV_CHUNK = 64
RWKV_GROUP_CHUNKS = 4
VMEM_LIMIT = 56 * 1024 * 1024

NN = (((1,), (0,)), ((), ()))
NT = (((1,), (1,)), ((), ()))
TN = (((0,), (0,)), ((), ()))
HI = lax.Precision.HIGHEST


def _dot(a, b, dims=NN, precision=None):
    return lax.dot_general(a, b, dims, precision=precision, preferred_element_type=F32)


def _mm(a, b, dims=NN):
    return _dot(a.astype(BF16), b.astype(BF16), dims)


def _split2(x):
    hi = x.astype(BF16)
    lo = (x - hi.astype(F32)).astype(BF16)
    return hi, lo


def _dot_exact_lhs(w_bf16, x):
    hi, lo = _split2(x)
    return _dot(w_bf16, hi) + _dot(w_bf16, lo)


def _sigmoid(x):
    return 1.0 / (1.0 + jnp.exp(-x))


def _rms(x, g):
    return x * lax.rsqrt(jnp.mean(x * x, axis=-1, keepdims=True) + NORM_EPS) * g


def _params(*sem):
    return pltpu.CompilerParams(dimension_semantics=sem, vmem_limit_bytes=VMEM_LIMIT)


def _in_proj_kernel(x_ref, g_ref, w_ref, o_ref, *, tn):
    h = _rms(x_ref[...], g_ref[...]).astype(BF16)
    for c in range(w_ref.shape[1] // tn):
        o_ref[:, c * tn:(c + 1) * tn] = _dot(h, w_ref[:, c * tn:(c + 1) * tn].astype(BF16))


def _in_proj(x2, g, w_all, layer, tm=1024, tn=512):
    m, d = x2.shape
    n = w_all.shape[2]
    return pl.pallas_call(
        functools.partial(_in_proj_kernel, tn=tn),
        grid=(m // tm,),
        in_specs=[
            pl.BlockSpec((tm, d), lambda i: (i, 0)),
            pl.BlockSpec((1, d), lambda i: (0, 0), pipeline_mode=pl.Buffered(1)),
            pl.BlockSpec((None, d, n), lambda i: (layer, 0, 0), pipeline_mode=pl.Buffered(1)),
        ],
        out_specs=pl.BlockSpec((tm, n), lambda i: (i, 0)),
        out_shape=jax.ShapeDtypeStruct((m, n), F32),
        compiler_params=_params("parallel"),
        name="in_proj",
    )(x2, g.reshape(1, d), w_all)


def _rwkv_kernel(ya_ref, mu_ref, wlo_ref, w0a0_ref, g2_ref, vecs_ref, tri_ref, hsum_ref, o_ref,
                 state_ref, prev_ref, at_ref, rt_ref, bh_ref, kh_ref, v_ref, cum_ref, y_ref, vr_ref, *, ts):
    nh, hd, c_len = RWKV_HEADS, HEAD_DIM, RWKV_CHUNK

    @pl.when(pl.program_id(1) == 0)
    def _():
        state_ref[...] = jnp.zeros_like(state_ref)
        prev_ref[...] = jnp.zeros_like(prev_ref)

    ya = ya_ref[0]
    row = lax.broadcasted_iota(jnp.int32, ya.shape, 0)
    prev = jnp.where(row == 0, prev_ref[0:1, :], pltpu.roll(ya, 1, axis=0))
    prev_ref[0:1, :] = ya[ts - 1:ts, :]
    xs = ya + mu_ref[...] * (prev - ya)

    r = xs[:, 0:D_RWKV]
    k = xs[:, D_RWKV:2 * D_RWKV]
    v = xs[:, 2 * D_RWKV:3 * D_RWKV]
    wa = xs[:, 3 * D_RWKV:3 * D_RWKV + LANE]
    gd = xs[:, 3 * D_RWKV + LANE:A_COLS]
    lane = lax.broadcasted_iota(jnp.int32, wa.shape, 1)
    wa = jnp.where(lane < DECAY_LORA, jnp.tanh(wa), wa)
    lo = _mm(wa, wlo_ref[...]) + w0a0_ref[...]
    z = -lo[:, 0:D_RWKV]
    softplus = jnp.maximum(z, 0.0) + jnp.log(1.0 + jnp.exp(-jnp.abs(z)))
    logdecay = -jnp.exp(-softplus - 0.5)
    alpha = _sigmoid(lo[:, D_RWKV:2 * D_RWKV])
    gate = _mm(_sigmoid(gd), g2_ref[...])

    k_k = vecs_ref[0:1, :]
    k_a = vecs_ref[1:2, :]
    r_k = vecs_ref[2:3, :]
    lnx_g = vecs_ref[3:4, :]
    lnx_b = vecs_ref[4:5, :]
    hsum = hsum_ref[...]

    kk = k * k_k
    kk = kk / jnp.maximum(jnp.sqrt(_mm(kk * kk, hsum)), 1e-12)
    k2 = k * (1.0 + (alpha - 1.0) * k_a)
    tri = tri_ref[...]
    tr = tri.shape[0]
    cum = jnp.concatenate([_dot_exact_lhs(tri, logdecay[i * tr:(i + 1) * tr]) for i in range(ts // tr)],
                          axis=0)
    e_neg = jnp.exp(-cum)
    at = -kk * jnp.exp(cum - logdecay)
    rt = r * jnp.exp(cum)
    bh = kk * alpha * e_neg
    kh = k2 * e_neg
    zeros_half = jnp.zeros((ts, hd), F32)
    for h in range(nh):
        sl = slice(h * hd, (h + 1) * hd)
        at_ref[h] = at[:, sl]
        rt_ref[h] = rt[:, sl]
        bh_ref[h] = bh[:, sl]
        kh_ref[h] = kh[:, sl]
        v_ref[h] = v[:, sl]
        vr_ref[h] = jnp.concatenate([zeros_half, v[:, sl]], axis=1)
        cum_ref[h] = cum[:, sl]

    ri = lax.broadcasted_iota(jnp.int32, (c_len, 2 * c_len), 0)
    ci = lax.broadcasted_iota(jnp.int32, (c_len, 2 * c_len), 1)
    ci = jnp.where(ci >= c_len, ci - c_len, ci)
    strict = ri > ci
    incl = ri >= ci
    eye = (lax.broadcasted_iota(jnp.int32, (c_len, c_len), 0)
           == lax.broadcasted_iota(jnp.int32, (c_len, c_len), 1))
    zeros_rows = jnp.zeros((c_len, hd), F32)
    n_sq = c_len.bit_length() - 1

    state = [state_ref[h] for h in range(nh)]

    def cut(ref, u):
        cc, h = u
        return ref[h, cc * c_len:(cc + 1) * c_len, :]

    for g0 in range(0, ts // c_len, RWKV_GROUP_CHUNKS):
        units = [(cc, h) for cc in range(g0, g0 + RWKV_GROUP_CHUNKS) for h in range(nh)]
        at_c = [cut(at_ref, u) for u in units]
        rt_c = [cut(rt_ref, u) for u in units]
        bh_c = [cut(bh_ref, u) for u in units]
        kh_c = [cut(kh_ref, u) for u in units]
        g_c = [jnp.exp(cum_ref[h, (cc + 1) * c_len - 1:(cc + 1) * c_len, :]) for cc, h in units]
        p_all = [_mm(jnp.concatenate([a, r_], axis=0), jnp.concatenate([b_, k_], axis=0).T)
                 for a, r_, b_, k_ in zip(at_c, rt_c, bh_c, kh_c)]
        top = [jnp.where(strict, pa[0:c_len], 0.0) for pa in p_all]
        bot = [jnp.where(incl, pa[c_len:], 0.0) for pa in p_all]
        x2 = [_mm(t, jnp.concatenate([zeros_rows, cut(v_ref, u)], axis=0)) for t, u in zip(top, units)]
        x = [jnp.concatenate([a, b_], axis=1) for a, b_ in zip(at_c, x2)]
        p = [t[:, 0:c_len] for t in top]
        for it in range(n_sq):
            if it + 1 < n_sq:
                px = [_mm(pi, jnp.concatenate([xi, pi], axis=1)) for pi, xi in zip(p, x)]
                x = [xi + r_[:, 0:2 * hd] for xi, r_ in zip(x, px)]
                p = [r_[:, 2 * hd:] for r_ in px]
            else:
                x = [xi + _mm(pi, xi) for pi, xi in zip(p, x)]
        rhs = [jnp.concatenate([xi, cut(vr_ref, u)], axis=0) for xi, u in zip(x, units)]
        res_y = [_mm(b_, r_) for b_, r_ in zip(bot, rhs)]
        res_m = [_mm(jnp.concatenate([b_ * g, k_ * g], axis=0), r_, TN)
                 for b_, k_, g, r_ in zip(bh_c, kh_c, g_c, rhs)]
        r_eff = [r_ + ry[:, 0:hd] for r_, ry in zip(rt_c, res_y)]
        g_mat = [jnp.where(eye, g, 0.0) + rm[:, 0:hd] for g, rm in zip(g_c, res_m)]
        for i, (cc, h) in enumerate(units):
            prod = _mm(jnp.concatenate([r_eff[i], g_mat[i]], axis=0), state[h])
            y_ref[h, cc * c_len:(cc + 1) * c_len, :] = prod[0:c_len] + res_y[i][:, hd:]
            state[h] = prod[c_len:] + res_m[i][:, hd:]
    for h in range(nh):
        state_ref[h] = state[h]

    y = jnp.concatenate([y_ref[h] for h in range(nh)], axis=-1)
    inv_hd = 1.0 / hd
    mean = _mm(y, hsum) * inv_hd
    yc = y - mean
    var = _mm(yc * yc, hsum) * inv_hd
    yn = yc * lax.rsqrt(var + RWKV_LNX_EPS) * lnx_g + lnx_b
    bonus = _mm(r * k2 * r_k, hsum) * v
    o_ref[0] = ((yn + bonus) * gate).astype(o_ref.dtype)


def _rwkv(proj, mu, w0, w2, a0, a2, g2, k_k, k_a, r_k, lnx_g, lnx_b, ts=512):
    b, s, _ = proj.shape
    wlo = jnp.zeros((LANE, 2 * D_RWKV), F32)
    wlo = wlo.at[0:DECAY_LORA, 0:D_RWKV].set(w2).at[DECAY_LORA:LANE, D_RWKV:].set(a2)
    w0a0 = jnp.concatenate([w0, a0]).reshape(1, 2 * D_RWKV)
    vecs = jnp.zeros((SUBLANE, D_RWKV), F32)
    vecs = vecs.at[0].set(k_k).at[1].set(k_a).at[2].set(r_k.reshape(-1)).at[3].set(lnx_g).at[4].set(lnx_b)
    tri_rows = min(ts, MXU_DEPTH)
    ti = jnp.arange(tri_rows)
    tri = ((ti[:, None] >= ti[None, :]) & (ti[:, None] // RWKV_CHUNK == ti[None, :] // RWKV_CHUNK)).astype(BF16)
    hi_ = jnp.arange(D_RWKV) // HEAD_DIM
    hsum = (hi_[:, None] == hi_[None, :]).astype(BF16)
    const = lambda shape: pl.BlockSpec(shape, lambda i, j: (0,) * len(shape))
    head_buf = pltpu.VMEM((RWKV_HEADS, ts, HEAD_DIM), F32)
    return pl.pallas_call(
        functools.partial(_rwkv_kernel, ts=ts),
        grid=(b, s // ts),
        in_specs=[
            pl.BlockSpec((1, ts, A_COLS), lambda i, j: (i, j, 0)),
            const((1, A_COLS)),
            const((LANE, 2 * D_RWKV)),
            const((1, 2 * D_RWKV)),
            const((GATE_LORA, D_RWKV)),
            const((SUBLANE, D_RWKV)),
            const((tri_rows, tri_rows)),
            const((D_RWKV, D_RWKV)),
        ],
        out_specs=pl.BlockSpec((1, ts, D_RWKV), lambda i, j: (i, j, 0)),
        out_shape=jax.ShapeDtypeStruct((b, s, D_RWKV), BF16),
        scratch_shapes=[
            pltpu.VMEM((RWKV_HEADS, HEAD_DIM, HEAD_DIM), F32),
            pltpu.VMEM((SUBLANE, A_COLS), F32),
            head_buf, head_buf, head_buf, head_buf, head_buf, head_buf, head_buf,
            pltpu.VMEM((RWKV_HEADS, ts, 2 * HEAD_DIM), F32),
        ],
        compiler_params=_params("parallel", "arbitrary"),
        name="rwkv7",
    )(proj, mu.reshape(1, A_COLS), wlo, w0a0, g2, vecs, tri, hsum)


def _sublane_allreduce(x, op):
    shift = SUBLANE // 2
    while shift:
        x = op(x, pltpu.roll(x, shift, axis=0))
        shift //= 2
    return x


def _moba_kernel(q_ref, k_ref, v_ref, o_ref, *, s):
    blk = MOBA_BLOCK
    nb = s // blk
    sub = SUBLANE
    q = q_ref[0]
    k = k_ref[0]
    k16 = k.astype(BF16)
    vt16 = jnp.concatenate([v_ref[0].T, jnp.ones((sub, s), F32)], axis=0).astype(BF16)
    lane = lax.broadcasted_iota(jnp.int32, (1, LANE), 1)
    kbar = jnp.concatenate(
        [jnp.mean(k[n * blk:(n + 1) * blk], axis=0, keepdims=True) for n in range(nb)], axis=0)
    key_i = lax.broadcasted_iota(jnp.int32, (blk, blk), 0)
    qry_i = lax.broadcasted_iota(jnp.int32, (blk, blk), 1)
    neg_inf = -jnp.inf
    causal_pen = jnp.where(key_i <= qry_i, 0.0, neg_inf).reshape(blk // sub, sub, blk)
    blk_id = lax.broadcasted_iota(jnp.int32, (nb, blk), 0)
    out_row = lax.broadcasted_iota(jnp.int32, (LANE, blk), 0)
    n_top = min(MOBA_TOPK, nb - 1)

    q_groups = [[qi for qi in range(nb) if qi % 4 in (0, 3)], [qi for qi in range(nb) if qi % 4 in (1, 2)]]
    for group in q_groups:
        units = [(qi, hh) for qi in group for hh in range(2)]
        logits = []
        for qi, hh in units:
            rows = slice(qi * blk, (qi + 1) * blk)
            head_lanes = (lane >= hh * HEAD_DIM) & (lane < (hh + 1) * HEAD_DIM)
            qh = jnp.where(head_lanes, q[rows], 0.0).T
            qh16 = (qh * (HEAD_DIM ** -0.5 * LOG2_E)).astype(BF16)
            first = 0 if n_top > 0 else qi
            kq = lambda n: _dot(k16[n * blk:(n + 1) * blk], qh16).reshape(blk // sub, sub, blk)
            lg3 = []
            if qi > first:
                sc = _dot(kbar, qh, NN, HI)
                cnt = jnp.zeros((nb, blk), F32)
                for m in range(qi):
                    cm = sc[m:m + 1, :]
                    beats = (cm > sc) | ((cm == sc) & (blk_id > m))
                    cnt = cnt + jnp.where(beats, 1.0, 0.0)
                pen = jnp.where((cnt < n_top) & (blk_id < qi), 0.0, neg_inf)
                for n in range(qi):
                    lg3.append(kq(n) + jnp.broadcast_to(pen[n:n + 1, :], (sub, blk))[None])
            lg3.append(kq(qi) + causal_pen)
            logits.append((first, lg3))
        m_rep = [_sublane_allreduce(functools.reduce(jnp.maximum, [jnp.max(lg, axis=0) for lg in lgs]),
                                    jnp.maximum) for _, lgs in logits]
        outs = []
        for (first, lgs), m_u in zip(logits, m_rep):
            acc = jnp.zeros((LANE + sub, blk), F32)
            for i, lg in enumerate(lgs):
                p = jnp.exp2(lg - m_u[None])
                kv = first + i
                acc = acc + _dot(vt16[:, kv * blk:(kv + 1) * blk], p.reshape(blk, blk).astype(BF16))
            l_rep = acc[LANE:]
            outs.append((acc[0:LANE].reshape(LANE // sub, sub, blk) / l_rep[None]).reshape(LANE, blk))
        for j, qi in enumerate(group):
            out_t = jnp.where(out_row < HEAD_DIM, outs[2 * j], outs[2 * j + 1])
            o_ref[0, qi * blk:(qi + 1) * blk, :] = out_t.T.astype(o_ref.dtype)


def _moba(proj):
    b, s, _ = proj.shape
    n_pair = D_MOBA // LANE
    base = A_COLS // LANE
    spec = lambda off: pl.BlockSpec((1, s, LANE), lambda i, p: (i, 0, base + off + p))
    return pl.pallas_call(
        functools.partial(_moba_kernel, s=s),
        grid=(b, n_pair),
        in_specs=[spec(0), spec(n_pair), spec(2 * n_pair)],
        out_specs=pl.BlockSpec((1, s, LANE), lambda i, p: (i, 0, p)),
        out_shape=jax.ShapeDtypeStruct((b, s, D_MOBA), BF16),
        compiler_params=_params("parallel", "parallel"),
        name="moba",
    )(proj, proj, proj)


def _gelu(x):
    return 0.5 * x * (1.0 + jnp.tanh(0.7978845608028654 * (x + 0.044715 * (x * x * x))))


def _gmlp_kernel(u_ref, v_ref, lng_ref, lnb_ref, ws_ref, bias_ref, o_ref, *, ts):
    ch = GMLP_CHUNK
    u = _gelu(u_ref[0])
    v = _gelu(v_ref[0])
    mean = jnp.mean(v, axis=-1, keepdims=True)
    vc = v - mean
    var = jnp.mean(vc * vc, axis=-1, keepdims=True)
    vn = ((vc * lax.rsqrt(var + NORM_EPS)) * lng_ref[...] + lnb_ref[...]).astype(BF16)
    ri = lax.broadcasted_iota(jnp.int32, (ch, ch), 0)
    ci = lax.broadcasted_iota(jnp.int32, (ch, ch), 1)
    tril = ri >= ci
    w = [jnp.where(tril, ws_ref[g], 0.0).astype(BF16) for g in range(GMLP_GROUPS)]
    lane = lax.broadcasted_iota(jnp.int32, (1, D_GMLP), 1)
    bias = bias_ref[...]
    for c in range(ts // ch):
        rows = slice(c * ch, (c + 1) * ch)
        vchunk = vn[rows]
        mixed = _dot(w[0], vchunk)
        for g in range(1, GMLP_GROUPS):
            mixed = jnp.where(lane >= g * HEAD_DIM, _dot(w[g], vchunk), mixed)
        o_ref[0, rows, :] = (u[rows] * (mixed + bias)).astype(o_ref.dtype)


def _gmlp(proj, ln_g, ln_b, w_s, b_s, ts=1024):
    b, s, _ = proj.shape
    base = (A_COLS + B_COLS) // D_GMLP
    bias = jnp.repeat(b_s.T, HEAD_DIM, axis=1)
    const = lambda shape: pl.BlockSpec(shape, lambda i, j: (0,) * len(shape))
    return pl.pallas_call(
        functools.partial(_gmlp_kernel, ts=ts),
        grid=(b, s // ts),
        in_specs=[
            pl.BlockSpec((1, ts, D_GMLP), lambda i, j: (i, j, base)),
            pl.BlockSpec((1, ts, D_GMLP), lambda i, j: (i, j, base + 1)),
            const((1, D_GMLP)),
            const((1, D_GMLP)),
            const((GMLP_GROUPS, GMLP_CHUNK, GMLP_CHUNK)),
            const((GMLP_CHUNK, D_GMLP)),
        ],
        out_specs=pl.BlockSpec((1, ts, D_GMLP), lambda i, j: (i, j, 0)),
        out_shape=jax.ShapeDtypeStruct((b, s, D_GMLP), BF16),
        compiler_params=_params("parallel", "parallel"),
        name="gmlp",
    )(proj, proj, ln_g.reshape(1, D_GMLP), ln_b.reshape(1, D_GMLP), w_s, bias)


def _out_proj_kernel(a_ref, b_ref, c_ref, w_ref, x_ref, g_ref, o_ref):
    mixed_in = jnp.concatenate([a_ref[...], b_ref[...], c_ref[...]], axis=-1)
    mix = _dot(mixed_in, w_ref[...].astype(BF16))
    o_ref[...] = x_ref[...] + _rms(mix, g_ref[...])


def _out_proj(a2, b2, c2, w_all, layer, x2, g, tm=1024):
    m, d = x2.shape
    row = lambda n: pl.BlockSpec((tm, n), lambda i: (i, 0))
    return pl.pallas_call(
        _out_proj_kernel,
        grid=(m // tm,),
        in_specs=[row(D_RWKV), row(D_MOBA), row(D_GMLP),
                  pl.BlockSpec((None, D_MODEL, d), lambda i: (layer, 0, 0), pipeline_mode=pl.Buffered(1)),
                  row(d), pl.BlockSpec((1, d), lambda i: (0, 0))],
        out_specs=row(d),
        out_shape=jax.ShapeDtypeStruct((m, d), F32),
        compiler_params=_params("parallel"),
        name="out_proj",
    )(a2, b2, c2, w_all, x2, g.reshape(1, d))


def _ffn_kernel(x_ref, gpre_ref, wi_ref, wo_ref, gpost_ref, o_ref, *, tf):
    x = x_ref[...]
    h = _rms(x, gpre_ref[...]).astype(BF16)
    f = None
    for c in range(D_FF // tf):
        gate = _dot(h, wi_ref[:, c * tf:(c + 1) * tf].astype(BF16))
        up = _dot(h, wi_ref[:, D_FF + c * tf:D_FF + (c + 1) * tf].astype(BF16))
        act = (gate * _sigmoid(gate) * up).astype(BF16)
        part = _dot(act, wo_ref[c * tf:(c + 1) * tf, :].astype(BF16))
        f = part if f is None else f + part
    o_ref[...] = x + _rms(f, gpost_ref[...])


def _ffn(x2, g_pre, w_in_all, w_out_all, layer, g_post, tm=512, tf=256):
    m, d = x2.shape
    resident = lambda shape: pl.BlockSpec(shape, lambda i: (0, 0), pipeline_mode=pl.Buffered(1))
    slab = lambda shape: pl.BlockSpec((None,) + shape, lambda i: (layer, 0, 0), pipeline_mode=pl.Buffered(1))
    return pl.pallas_call(
        functools.partial(_ffn_kernel, tf=tf),
        grid=(m // tm,),
        in_specs=[
            pl.BlockSpec((tm, d), lambda i: (i, 0)),
            resident((1, d)),
            slab((d, 2 * D_FF)),
            slab((D_FF, d)),
            resident((1, d)),
        ],
        out_specs=pl.BlockSpec((tm, d), lambda i: (i, 0)),
        out_shape=jax.ShapeDtypeStruct((m, d), F32),
        compiler_params=_params("parallel"),
        name="ffn",
    )(x2, g_pre.reshape(1, d), w_in_all, w_out_all, g_post.reshape(1, d))


def kernel(x, pre_mix_g, w_in, rwkv_mu, rwkv_w0, rwkv_w2, rwkv_a0, rwkv_a2, rwkv_g2, rwkv_k_k, rwkv_k_a, rwkv_r_k, rwkv_lnx_g, rwkv_lnx_b, gmlp_ln_g, gmlp_ln_b, gmlp_w_s, gmlp_b_s, w_out, post_mix_g, pre_ffn_g, w_ffn_in, w_ffn_out, post_ffn_g):
    b, s, d = x.shape
    m = b * s
    x2 = x.reshape(m, d)
    for l in range(w_in.shape[0]):
        proj = _in_proj(x2, pre_mix_g[l], w_in, l).reshape(b, s, D_IN)
        a_out = _rwkv(proj, rwkv_mu[l], rwkv_w0[l], rwkv_w2[l], rwkv_a0[l], rwkv_a2[l], rwkv_g2[l],
                      rwkv_k_k[l], rwkv_k_a[l], rwkv_r_k[l], rwkv_lnx_g[l], rwkv_lnx_b[l])
        b_out = _moba(proj)
        c_out = _gmlp(proj, gmlp_ln_g[l], gmlp_ln_b[l], gmlp_w_s[l], gmlp_b_s[l])
        x2 = _out_proj(a_out.reshape(m, D_RWKV), b_out.reshape(m, D_MOBA), c_out.reshape(m, D_GMLP),
                       w_out, l, x2, post_mix_g[l])
        x2 = _ffn(x2, pre_ffn_g[l], w_ffn_in, w_ffn_out, l, post_ffn_g[l])
    return x2.reshape(b, s, d)
```
